```python
import jax, jax.numpy as jnp
from jax import lax
import numpy as np

D_MODEL = 1024
BATCH = 32
SEQ = 2048
DEPTH = 2

RET_HEADS = 4
RET_DK = 128
RET_DV = 256
RET_QK = RET_HEADS * RET_DK
RET_V = RET_HEADS * RET_DV
RET_CHUNK = 128
ROPE_BASE = 10000.0
SB_HEADS = 8
SB_DH = 64
SB_W = SB_HEADS * SB_DH
SB_BLOCK = 128
FF = ((8 * D_MODEL + 3 * 256 - 1) // (3 * 256)) * 256
NORM_EPS = 1e-6
IN_SPLITS = (RET_QK, RET_QK, RET_V, RET_V, SB_W, SB_W, SB_W, D_MODEL, D_MODEL)
IN_COLS = sum(IN_SPLITS)

kernel_name = "hybrid_retention_stickbreaking_gated"


def _rmsnorm(x, g):
    x32 = x.astype(jnp.float32)
    y = x32 * lax.rsqrt(jnp.mean(x32 * x32, axis=-1, keepdims=True) + NORM_EPS)
    return (y * g.astype(jnp.float32)).astype(x.dtype)


def _split_cols(p):
    offs = np.cumsum(IN_SPLITS)[:-1].tolist()
    return jnp.split(p, offs, axis=-1)


def _rope(x):
    S, d = x.shape[1], x.shape[-1]
    half = d // 2
    pos = jnp.arange(S, dtype=jnp.float32)
    inv = 1.0 / (ROPE_BASE ** (jnp.arange(half, dtype=jnp.float32) / half))
    ang = pos[:, None] * inv[None, :]
    cos = jnp.cos(ang)[None, :, None, :]
    sin = jnp.sin(ang)[None, :, None, :]
    x32 = x.astype(jnp.float32)
    x1, x2 = x32[..., :half], x32[..., half:]
    return jnp.concatenate([x1 * cos - x2 * sin, x1 * sin + x2 * cos], axis=-1)


def _retention_chunkwise(q, k, v):
    B, S, H, DK = q.shape
    DV = v.shape[-1]
    C = RET_CHUNK
    N = S // C
    log_g = jnp.log1p(-jnp.exp2(-5.0 - jnp.arange(H, dtype=jnp.float32)))
    q = q.reshape(B, N, C, H, DK)
    k = k.reshape(B, N, C, H, DK)
    v = v.reshape(B, N, C, H, DV)
    i = jnp.arange(C, dtype=jnp.float32)
    diff = i[:, None] - i[None, :]
    dec = jnp.where(diff[None] >= 0,
                    jnp.exp(log_g[:, None, None] * jnp.maximum(diff, 0.0)[None]), 0.0)
    scores = jnp.einsum('bnihd,bnjhd->bnhij', q, k) * dec[None, None]
    o_intra = jnp.einsum('bnhij,bnjhe->bnihe', scores, v)
    k_dec = jnp.exp(log_g[:, None] * (C - 1.0 - i)[None, :])
    kv = jnp.einsum('bnjhd,hj,bnjhe->nbhde', k, k_dec, v)
    chunk_dec = jnp.exp(log_g * C)[None, :, None, None]

    def step(R, kv_n):
        return chunk_dec * R + kv_n, R

    _, R_prev = lax.scan(step, jnp.zeros((B, H, DK, DV), jnp.float32), kv)
    q_dec = jnp.exp(log_g[:, None] * (i + 1.0)[None, :])
    o_cross = jnp.einsum('bnihd,nbhde,hi->bnihe', q, R_prev, q_dec)
    return (o_intra + o_cross).reshape(B, S, H, DV)


def _head_norm(y):
    mu = jnp.mean(y, axis=-1, keepdims=True)
    var = jnp.mean(jnp.square(y - mu), axis=-1, keepdims=True)
    return (y - mu) * lax.rsqrt(var + NORM_EPS)


def _stick_breaking(q, k, v):
    B, S, H, d = q.shape
    scale = d ** -0.5
    outs = []
    for blk in range(S // SB_BLOCK):
        start, end = blk * SB_BLOCK, (blk + 1) * SB_BLOCK
        qb = q[:, start:end]
        kk = k[:, :end]
        vv = v[:, :end]
        z = jnp.einsum('bqhd,bkhd->bhqk', qb, kk).astype(jnp.float32) * scale
        q_pos = start + jnp.arange(SB_BLOCK)
        k_pos = jnp.arange(end)
        mask = (k_pos[None, :] < q_pos[:, None])[None, None]
        log_beta = jax.nn.log_sigmoid(z)
        log_one_minus = jnp.where(mask, jax.nn.log_sigmoid(-z), 0.0)
        suffix = lax.cumsum(log_one_minus, axis=3, reverse=True) - log_one_minus
        A = jnp.where(mask, jnp.exp(log_beta + suffix), 0.0)
        outs.append(jnp.einsum('bhqk,bkhd->bqhd', A.astype(v.dtype), vv))
    return jnp.concatenate(outs, axis=1)


def _mixer(h, w_in, w_ret_o, w_sb_o, w_out):
    B, S, _ = h.shape
    proj = jnp.einsum('bsd,df->bsf', h, w_in)
    rq, rk, rv, rg, sq, sk, sv, gr, gs = _split_cols(proj)
    rq = _rope(rq.reshape(B, S, RET_HEADS, RET_DK))
    rk = _rope(rk.reshape(B, S, RET_HEADS, RET_DK)) * (RET_DK ** -0.5)
    rv = rv.reshape(B, S, RET_HEADS, RET_DV).astype(jnp.float32)
    ret = _head_norm(_retention_chunkwise(rq, rk, rv)).reshape(B, S, RET_V).astype(h.dtype)
    ret = ret * jax.nn.silu(rg)
    ret = jnp.einsum('bsv,vd->bsd', ret, w_ret_o)
    sb = _stick_breaking(sq.reshape(B, S, SB_HEADS, SB_DH),
                         sk.reshape(B, S, SB_HEADS, SB_DH),
                         sv.reshape(B, S, SB_HEADS, SB_DH)).reshape(B, S, SB_W)
    sb = jnp.einsum('bsv,vd->bsd', sb, w_sb_o)
    merged = jax.nn.sigmoid(gr) * ret + jax.nn.sigmoid(gs) * sb
    return jnp.einsum('bsd,de->bse', merged, w_out)


def _swiglu(h, w_gate_up, w_down):
    gu = jnp.einsum('bsd,df->bsf', h, w_gate_up)
    g, u = gu[..., :FF], gu[..., FF:]
    return jnp.einsum('bsf,fd->bsd', jax.nn.silu(g) * u, w_down)


def setup_inputs(seed: int = 0) -> dict:
    key = jax.random.key(seed)
    ks = jax.random.split(key, 10)
    f32 = jnp.float32

    def w(k, shape, fan_in):
        return jax.random.normal(k, shape, f32) * (fan_in ** -0.5)

    return {
        "x": jax.random.normal(ks[0], (BATCH, SEQ, D_MODEL), f32),
        "norm_mix": 1.0 + 0.02 * jax.random.normal(ks[1], (DEPTH, D_MODEL), f32),
        "w_in": w(ks[2], (DEPTH, D_MODEL, IN_COLS), D_MODEL),
        "w_ret_o": w(ks[3], (DEPTH, RET_V, D_MODEL), RET_V),
        "w_sb_o": w(ks[4], (DEPTH, SB_W, D_MODEL), SB_W),
        "w_out": w(ks[5], (DEPTH, D_MODEL, D_MODEL), D_MODEL),
        "norm_ffn": 1.0 + 0.02 * jax.random.normal(ks[6], (DEPTH, D_MODEL), f32),
        "w_gate_up": w(ks[7], (DEPTH, D_MODEL, 2 * FF), D_MODEL),
        "w_down": w(ks[8], (DEPTH, FF, D_MODEL), FF),
        "norm_final": 1.0 + 0.02 * jax.random.normal(ks[9], (D_MODEL,), f32),
    }


def reference(x, norm_mix, w_in, w_ret_o, w_sb_o, w_out, norm_ffn, w_gate_up, w_down, norm_final):
    h = x
    for layer in range(DEPTH):
        hn = _rmsnorm(h, norm_mix[layer])
        h = h + _mixer(hn, w_in[layer], w_ret_o[layer], w_sb_o[layer], w_out[layer])
        hn = _rmsnorm(h, norm_ffn[layer])
        h = h + _swiglu(hn, w_gate_up[layer], w_down[layer])
    return _rmsnorm(h, norm_final)
```

```python
import functools

import jax
import jax.numpy as jnp
from jax import lax
from jax.experimental import pallas as pl
from jax.experimental.pallas import tpu as pltpu

F32 = jnp.float32
BF16 = jnp.bfloat16

D_MODEL = 1024
RET_HEADS = 4
RET_DK = 128
RET_DV = 256
RET_QK = RET_HEADS * RET_DK
RET_V = RET_HEADS * RET_DV
RET_CHUNK = 128
ROPE_BASE = 10000.0
SB_HEADS = 8
SB_DH = 64
SB_W = SB_HEADS * SB_DH
SB_BLOCK = 128
FF = 2816
NORM_EPS = 1e-6
IN_COLS = 2 * RET_QK + 2 * RET_V + 3 * SB_W + 2 * D_MODEL

OFF_RQ = 0
OFF_RK = OFF_RQ + RET_QK
OFF_RV = OFF_RK + RET_QK
OFF_RG = OFF_RV + RET_V
OFF_SQ = OFF_RG + RET_V
OFF_SK = OFF_SQ + SB_W
OFF_SV = OFF_SK + SB_W
OFF_GR = OFF_SV + SB_W
OFF_GS = OFF_GR + D_MODEL

LANES = 128
PROJ_TN = 512
FFN_TM = 1024
FFN_TF = 256
MERGE_TM = 512
VMEM_LIMIT = 56 * 1024 * 1024

_NT = (((1,), (1,)), ((), ()))
_TN = (((0,), (0,)), ((), ()))


def _sigmoid(x):
    return 1.0 / (1.0 + jnp.exp(-x))


def _rmsnorm_rows(x, g):
    ms = jnp.mean(x * x, axis=-1, keepdims=True)
    return x * lax.rsqrt(ms + NORM_EPS) * g


def _inproj_kernel(h_ref, g_ref, w_ref, cos_ref, sin_ref, o_ref, hn_ref):
    j = pl.program_id(1)

    @pl.when(j == 0)
    def _():
        hn_ref[...] = _rmsnorm_rows(h_ref[...], g_ref[...]).astype(BF16)

    y = jnp.dot(hn_ref[...], w_ref[...], preferred_element_type=F32)

    j_rk = OFF_RK // PROJ_TN
    j_rv = OFF_RV // PROJ_TN
    j_rg = OFF_RG // PROJ_TN
    j_sq = OFF_SQ // PROJ_TN
    j_sk = OFF_SK // PROJ_TN
    j_gr = OFF_GR // PROJ_TN

    @pl.when(j < j_rv)
    def _():
        scale = jnp.where(j == j_rk, RET_DK ** -0.5, 1.0).astype(F32)
        cos = cos_ref[...]
        sin = sin_ref[...]
        for hd in range(PROJ_TN // RET_DK):
            sl = slice(hd * RET_DK, (hd + 1) * RET_DK)
            yh = y[:, sl]
            r = yh * cos + pltpu.roll(yh, RET_DK // 2, 1) * sin
            o_ref[:, sl] = (r * scale).astype(BF16)

    @pl.when(((j >= j_rv) & (j < j_rg)) | ((j >= j_sk) & (j < j_gr)))
    def _():
        o_ref[...] = y.astype(BF16)

    @pl.when((j >= j_rg) & (j < j_sq))
    def _():
        o_ref[...] = (y * _sigmoid(y)).astype(BF16)

    @pl.when(j == j_sq)
    def _():
        o_ref[...] = (y * (SB_DH ** -0.5)).astype(BF16)

    @pl.when(j >= j_gr)
    def _():
        o_ref[...] = _sigmoid(y).astype(BF16)


def _inproj(h, g, w_in, cos, sin):
    B, S, D = h.shape
    return pl.pallas_call(
        _inproj_kernel,
        out_shape=jax.ShapeDtypeStruct((B, S, IN_COLS), BF16),
        grid=(B, IN_COLS // PROJ_TN),
        in_specs=[
            pl.BlockSpec((None, S, D), lambda b, j: (b, 0, 0)),
            pl.BlockSpec((1, D), lambda b, j: (0, 0)),
            pl.BlockSpec((D, PROJ_TN), lambda b, j: (0, j)),
            pl.BlockSpec((S, RET_DK), lambda b, j: (0, 0)),
            pl.BlockSpec((S, RET_DK), lambda b, j: (0, 0)),
        ],
        out_specs=pl.BlockSpec((None, S, PROJ_TN), lambda b, j: (b, 0, j)),
        scratch_shapes=[pltpu.VMEM((S, D), BF16)],
        compiler_params=pltpu.CompilerParams(
            dimension_semantics=("parallel", "arbitrary"), vmem_limit_bytes=VMEM_LIMIT),
        name="inproj",
    )(h, g, w_in, cos, sin)


def _retention_kernel(lg_ref, q_ref, k_ref, v_ref, g_ref, o_ref, r_ref):
    C = RET_CHUNK
    n_chunks = q_ref.shape[0] // C
    lg_row = lg_ref[...]
    lg = lg_row[:, :1]
    ii = lax.broadcasted_iota(jnp.int32, (C, C), 0)
    jj = lax.broadcasted_iota(jnp.int32, (C, C), 1)
    diff = (ii - jj).astype(F32)
    dec = jnp.where(diff >= 0, jnp.exp(lg_row * jnp.maximum(diff, 0.0)), 0.0)
    i_col = lax.broadcasted_iota(jnp.int32, (C, 1), 0).astype(F32)
    k_dec = jnp.exp(lg * (C - 1.0 - i_col))
    q_dec = jnp.exp(lg * (i_col + 1.0))
    chunk_dec = jnp.exp(lg * C)

    r_ref[...] = jnp.zeros_like(r_ref)

    def body(n, carry):
        rows = pl.ds(pl.multiple_of(n * C, C), C)
        q = q_ref[rows, :]
        k = k_ref[rows, :]
        v = v_ref[rows, :]
        s = lax.dot_general(q, k, _NT, preferred_element_type=F32) * dec
        o = jnp.dot(s.astype(BF16), v, preferred_element_type=F32)
        r = r_ref[...]
        o = o + jnp.dot(q, r.astype(BF16), preferred_element_type=F32) * q_dec
        kd = (k.astype(F32) * k_dec).astype(BF16)
        kv = lax.dot_general(kd, v, _TN, preferred_element_type=F32)
        r_ref[...] = chunk_dec * r + kv
        mu = jnp.mean(o, axis=-1, keepdims=True)
        yc = o - mu
        var = jnp.mean(yc * yc, axis=-1, keepdims=True)
        y = yc * lax.rsqrt(var + NORM_EPS)
        o_ref[rows, :] = (y * g_ref[rows, :].astype(F32)).astype(BF16)
        return carry

    lax.fori_loop(0, n_chunks, body, 0)


def _retention(proj, log_g):
    B, S, _ = proj.shape
    qb, kb = OFF_RQ // RET_DK, OFF_RK // RET_DK
    vb, gb = OFF_RV // RET_DV, OFF_RG // RET_DV
    return pl.pallas_call(
        _retention_kernel,
        out_shape=jax.ShapeDtypeStruct((B, S, RET_V), BF16),
        grid=(B, RET_HEADS),
        in_specs=[
            pl.BlockSpec((None, 1, LANES), lambda b, h: (h, 0, 0)),
            pl.BlockSpec((None, S, RET_DK), lambda b, h: (b, 0, qb + h)),
            pl.BlockSpec((None, S, RET_DK), lambda b, h: (b, 0, kb + h)),
            pl.BlockSpec((None, S, RET_DV), lambda b, h: (b, 0, vb + h)),
            pl.BlockSpec((None, S, RET_DV), lambda b, h: (b, 0, gb + h)),
        ],
        out_specs=pl.BlockSpec((None, S, RET_DV), lambda b, h: (b, 0, h)),
        scratch_shapes=[pltpu.VMEM((RET_DK, RET_DV), F32)],
        compiler_params=pltpu.CompilerParams(
            dimension_semantics=("parallel", "parallel"), vmem_limit_bytes=VMEM_LIMIT),
        name="retention",
    )(log_g, proj, proj, proj, proj)


def _sb_kernel(q_ref, k_ref, v_ref, t_ref, o_ref):
    BLK = SB_BLOCK
    n_blocks = q_ref.shape[0] // BLK
    lane = lax.broadcasted_iota(jnp.int32, (BLK, LANES), 1)
    head0 = lane < SB_DH
    qi_loc = lax.broadcasted_iota(jnp.int32, (2 * BLK, BLK), 0) % BLK
    ki_loc = lax.broadcasted_iota(jnp.int32, (2 * BLK, BLK), 1)
    diag_mask = ki_loc < qi_loc
    tri = t_ref[...]

    def tile(q2, rows_k, carry, acc, masked):
        kj = k_ref[rows_k, :]
        vj = v_ref[rows_k, :]
        zero = jnp.zeros_like(vj)
        v2 = jnp.concatenate([jnp.where(head0, vj, zero), jnp.where(head0, zero, vj)], axis=0)
        z = lax.dot_general(q2, kj, _NT, preferred_element_type=F32)
        sp = jnp.maximum(z, 0.0) + jnp.log(1.0 + jnp.exp(-jnp.abs(z)))
        ls = z - sp
        if masked:
            sp = jnp.where(diag_mask, sp, 0.0)
        hi = sp.astype(BF16)
        lo = (sp - hi.astype(F32)).astype(BF16)
        suffix = jnp.dot(jnp.concatenate([hi, lo], axis=1), tri, preferred_element_type=F32)
        a = jnp.exp(ls - suffix - carry)
        if masked:
            a = jnp.where(diag_mask, a, 0.0)
        ab = a.astype(BF16)
        a2 = jnp.concatenate([ab[:BLK], ab[BLK:]], axis=1)
        acc = acc + jnp.dot(a2, v2, preferred_element_type=F32)
        carry = carry + jnp.sum(sp, axis=1, keepdims=True)
        return carry, acc

    def q_block(qi, _):
        rows_q = pl.ds(pl.multiple_of(qi * BLK, BLK), BLK)
        q = q_ref[rows_q, :]
        zero = jnp.zeros_like(q)
        q2 = jnp.concatenate([jnp.where(head0, q, zero), jnp.where(head0, zero, q)], axis=0)
        carry = jnp.zeros((2 * BLK, 1), F32)
        acc = jnp.zeros((BLK, LANES), F32)
        carry, acc = tile(q2, rows_q, carry, acc, True)

        def k_block(t, state):
            rows_k = pl.ds(pl.multiple_of((qi - 1 - t) * BLK, BLK), BLK)
            return tile(q2, rows_k, state[0], state[1], False)

        carry, acc = lax.fori_loop(0, qi, k_block, (carry, acc))
        o_ref[rows_q, :] = acc.astype(BF16)
        return 0

    lax.fori_loop(0, n_blocks, q_block, 0)


def _stick_breaking(proj, tri):
    B, S, _ = proj.shape
    qb, kb, vb = OFF_SQ // LANES, OFF_SK // LANES, OFF_SV // LANES
    n_pairs = SB_W // LANES
    return pl.pallas_call(
        _sb_kernel,
        out_shape=jax.ShapeDtypeStruct((B, S, SB_W), BF16),
        grid=(B, n_pairs),
        in_specs=[
            pl.BlockSpec((None, S, LANES), lambda b, p: (b, 0, qb + p)),
            pl.BlockSpec((None, S, LANES), lambda b, p: (b, 0, kb + p)),
            pl.BlockSpec((None, S, LANES), lambda b, p: (b, 0, vb + p)),
            pl.BlockSpec((2 * SB_BLOCK, SB_BLOCK), lambda b, p: (0, 0)),
        ],
        out_specs=pl.BlockSpec((None, S, LANES), lambda b, p: (b, 0, p)),
        compiler_params=pltpu.CompilerParams(
            dimension_semantics=("parallel", "parallel"), vmem_limit_bytes=VMEM_LIMIT),
        name="stick_breaking",
    )(proj, proj, proj, tri)


def _merge_kernel(h_ref, ret_ref, sb_ref, gr0_ref, gr1_ref, gs0_ref, gs1_ref,
                  wr_ref, ws_ref, wo_ref, o_ref):
    ro = jnp.dot(ret_ref[...], wr_ref[...], preferred_element_type=F32)
    so = jnp.dot(sb_ref[...], ws_ref[...], preferred_element_type=F32)
    gr = jnp.concatenate([gr0_ref[...], gr1_ref[...]], axis=1).astype(F32)
    gs = jnp.concatenate([gs0_ref[...], gs1_ref[...]], axis=1).astype(F32)
    merged = (gr * ro + gs * so).astype(BF16)
    o_ref[...] = h_ref[...] + jnp.dot(merged, wo_ref[...], preferred_element_type=F32)


def _merge(h, ret, sb, proj, w_ret_o, w_sb_o, w_out):
    B, S, D = h.shape
    tm = min(MERGE_TM, S)
    half = D_MODEL // 2
    gr0, gs0 = OFF_GR // half, OFF_GS // half
    tok = lambda width, col: pl.BlockSpec((None, tm, width), lambda b, i: (b, i, col))
    full = lambda a: pl.BlockSpec(a.shape, lambda b, i: (0, 0))
    return pl.pallas_call(
        _merge_kernel,
        out_shape=jax.ShapeDtypeStruct((B, S, D), F32),
        grid=(B, S // tm),
        in_specs=[tok(D, 0), tok(RET_V, 0), tok(SB_W, 0),
                  tok(half, gr0), tok(half, gr0 + 1), tok(half, gs0), tok(half, gs0 + 1),
                  full(w_ret_o), full(w_sb_o), full(w_out)],
        out_specs=tok(D, 0),
        compiler_params=pltpu.CompilerParams(
            dimension_semantics=("parallel", "parallel"), vmem_limit_bytes=VMEM_LIMIT),
        name="merge",
    )(h, ret, sb, proj, proj, proj, proj, w_ret_o, w_sb_o, w_out)


def _ffn_kernel(h_ref, g_ref, wg_ref, wu_ref, wd_ref, gf_ref, o_ref, hn_ref, *, final):
    f = pl.program_id(1)

    @pl.when(f == 0)
    def _():
        x = h_ref[...]
        hn_ref[...] = _rmsnorm_rows(x, g_ref[...]).astype(BF16)
        o_ref[...] = x

    hn = hn_ref[...]
    g = jnp.dot(hn, wg_ref[...], preferred_element_type=F32)
    u = jnp.dot(hn, wu_ref[...], preferred_element_type=F32)
    act = (g * _sigmoid(g) * u).astype(BF16)
    o_ref[...] += jnp.dot(act, wd_ref[...], preferred_element_type=F32)

    if final:
        @pl.when(f == pl.num_programs(1) - 1)
        def _():
            o_ref[...] = _rmsnorm_rows(o_ref[...], gf_ref[...])


def _ffn(h, g, w_gate_up, w_down, g_final, final):
    B, S, D = h.shape
    T = B * S
    tm = min(FFN_TM, T)
    h2 = h.reshape(T, D)
    nf = FF // FFN_TF
    out = pl.pallas_call(
        functools.partial(_ffn_kernel, final=final),
        out_shape=jax.ShapeDtypeStruct((T, D), F32),
        grid=(T // tm, nf),
        in_specs=[
            pl.BlockSpec((tm, D), lambda i, f: (i, 0)),
            pl.BlockSpec((1, D), lambda i, f: (0, 0)),
            pl.BlockSpec((D, FFN_TF), lambda i, f: (0, f)),
            pl.BlockSpec((D, FFN_TF), lambda i, f: (0, nf + f)),
            pl.BlockSpec((FFN_TF, D), lambda i, f: (f, 0)),
            pl.BlockSpec((1, D), lambda i, f: (0, 0)),
        ],
        out_specs=pl.BlockSpec((tm, D), lambda i, f: (i, 0)),
        scratch_shapes=[pltpu.VMEM((tm, D), BF16)],
        compiler_params=pltpu.CompilerParams(
            dimension_semantics=("parallel", "arbitrary"), vmem_limit_bytes=VMEM_LIMIT),
        name="ffn_final" if final else "ffn",
    )(h2, g, w_gate_up, w_gate_up, w_down, g_final)
    return out.reshape(B, S, D)


def _rope_tables(S):
    half = RET_DK // 2
    pos = jnp.arange(S, dtype=F32)
    inv = 1.0 / (ROPE_BASE ** (jnp.arange(half, dtype=F32) / half))
    ang = pos[:, None] * inv[None, :]
    cos = jnp.cos(ang)
    sin = jnp.sin(ang)
    return jnp.concatenate([cos, cos], axis=-1), jnp.concatenate([-sin, sin], axis=-1)


def kernel(x, norm_mix, w_in, w_ret_o, w_sb_o, w_out, norm_ffn, w_gate_up, w_down, norm_final):
    B, S, D = x.shape
    depth = w_in.shape[0]
    cos, sin = _rope_tables(S)
    log_g = jnp.log1p(-jnp.exp2(-5.0 - jnp.arange(RET_HEADS, dtype=F32)))
    log_g = jnp.broadcast_to(log_g[:, None, None], (RET_HEADS, 1, LANES))
    r = jnp.arange(2 * SB_BLOCK)[:, None] % SB_BLOCK
    c = jnp.arange(SB_BLOCK)[None, :]
    tri = (r > c).astype(BF16)
    g_final = norm_final.reshape(1, D)

    h = x
    for layer in range(depth):
        proj = _inproj(h, norm_mix[layer].reshape(1, D), w_in[layer].astype(BF16), cos, sin)
        ret = _retention(proj, log_g)
        sb = _stick_breaking(proj, tri)
        h = _merge(h, ret, sb, proj, w_ret_o[layer].astype(BF16), w_sb_o[layer].astype(BF16),
                   w_out[layer].astype(BF16))
        h = _ffn(h, norm_ffn[layer].reshape(1, D), w_gate_up[layer].astype(BF16),
                 w_down[layer].astype(BF16), g_final, final=(layer == depth - 1))
    return h
```

```python
import functools

import jax
import jax.numpy as jnp
from jax import lax
from jax.experimental import pallas as pl
from jax.experimental.pallas import tpu as pltpu

F32 = jnp.float32
BF16 = jnp.bfloat16

D_MODEL = 1024
RET_HEADS = 4
RET_DK = 128
RET_DV = 256
RET_QK = RET_HEADS * RET_DK
RET_V = RET_HEADS * RET_DV
RET_CHUNK = 128
ROPE_BASE = 10000.0
SB_HEADS = 8
SB_DH = 64
SB_W = SB_HEADS * SB_DH
SB_BLOCK = 128
FF = 2816
NORM_EPS = 1e-6
IN_COLS = 2 * RET_QK + 2 * RET_V + 3 * SB_W + 2 * D_MODEL

OFF_RQ = 0
OFF_RK = OFF_RQ + RET_QK
OFF_RV = OFF_RK + RET_QK
OFF_RG = OFF_RV + RET_V
OFF_SQ = OFF_RG + RET_V
OFF_SK = OFF_SQ + SB_W
OFF_SV = OFF_SK + SB_W
OFF_GR = OFF_SV + SB_W
OFF_GS = OFF_GR + D_MODEL

LANES = 128
PROJ_TN = 512
FFN_TM = 1024
FFN_TF = 256
MERGE_TM = 512
VMEM_LIMIT = 56 * 1024 * 1024

_NT = (((1,), (1,)), ((), ()))
_TN = (((0,), (0,)), ((), ()))


def _sigmoid(x):
    return 1.0 / (1.0 + jnp.exp(-x))


def _rmsnorm_rows(x, g):
    ms = jnp.mean(x * x, axis=-1, keepdims=True)
    return x * lax.rsqrt(ms + NORM_EPS) * g


def _inproj_kernel(h_ref, g_ref, w_ref, cos_ref, sin_ref, o_ref, hn_ref):
    j = pl.program_id(1)

    @pl.when(j == 0)
    def _():
        hn_ref[...] = _rmsnorm_rows(h_ref[...], g_ref[...]).astype(BF16)

    y = jnp.dot(hn_ref[...], w_ref[...], preferred_element_type=F32)

    j_rk = OFF_RK // PROJ_TN
    j_rv = OFF_RV // PROJ_TN
    j_rg = OFF_RG // PROJ_TN
    j_sq = OFF_SQ // PROJ_TN
    j_sk = OFF_SK // PROJ_TN
    j_gr = OFF_GR // PROJ_TN

    @pl.when(j < j_rv)
    def _():
        scale = jnp.where(j == j_rk, RET_DK ** -0.5, 1.0).astype(F32)
        cos = cos_ref[...]
        sin = sin_ref[...]
        for hd in range(PROJ_TN // RET_DK):
            sl = slice(hd * RET_DK, (hd + 1) * RET_DK)
            yh = y[:, sl]
            r = yh * cos + pltpu.roll(yh, RET_DK // 2, 1) * sin
            o_ref[:, sl] = (r * scale).astype(BF16)

    @pl.when(((j >= j_rv) & (j < j_rg)) | ((j >= j_sk) & (j < j_gr)))
    def _():
        o_ref[...] = y.astype(BF16)

    @pl.when((j >= j_rg) & (j < j_sq))
    def _():
        o_ref[...] = (y * _sigmoid(y)).astype(BF16)

    @pl.when(j == j_sq)
    def _():
        o_ref[...] = (y * (SB_DH ** -0.5)).astype(BF16)

    @pl.when(j >= j_gr)
    def _():
        o_ref[...] = _sigmoid(y).astype(BF16)


def _inproj(h, g, w_in, cos, sin):
    B, S, D = h.shape
    return pl.pallas_call(
        _inproj_kernel,
        out_shape=jax.ShapeDtypeStruct((B, S, IN_COLS), BF16),
        grid=(B, IN_COLS // PROJ_TN),
        in_specs=[
            pl.BlockSpec((None, S, D), lambda b, j: (b, 0, 0)),
            pl.BlockSpec((1, D), lambda b, j: (0, 0)),
            pl.BlockSpec((D, PROJ_TN), lambda b, j: (0, j)),
            pl.BlockSpec((S, RET_DK), lambda b, j: (0, 0)),
            pl.BlockSpec((S, RET_DK), lambda b, j: (0, 0)),
        ],
        out_specs=pl.BlockSpec((None, S, PROJ_TN), lambda b, j: (b, 0, j)),
        scratch_shapes=[pltpu.VMEM((S, D), BF16)],
        compiler_params=pltpu.CompilerParams(
            dimension_semantics=("parallel", "arbitrary"), vmem_limit_bytes=VMEM_LIMIT),
        name="inproj",
    )(h, g, w_in, cos, sin)


def _retention_kernel(lg_ref, q_ref, k_ref, v_ref, g_ref, o_ref, r_ref):
    C = RET_CHUNK
    n_chunks = q_ref.shape[0] // C
    lg_row = lg_ref[...]
    lg = lg_row[:, :1]
    ii = lax.broadcasted_iota(jnp.int32, (C, C), 0)
    jj = lax.broadcasted_iota(jnp.int32, (C, C), 1)
    diff = (ii - jj).astype(F32)
    dec = jnp.where(diff >= 0, jnp.exp(lg_row * jnp.maximum(diff, 0.0)), 0.0)
    i_col = lax.broadcasted_iota(jnp.int32, (C, 1), 0).astype(F32)
    k_dec = jnp.exp(lg * (C - 1.0 - i_col))
    q_dec = jnp.exp(lg * (i_col + 1.0))
    chunk_dec = jnp.exp(lg * C)

    r_ref[...] = jnp.zeros_like(r_ref)

    def body(n, carry):
        rows = pl.ds(pl.multiple_of(n * C, C), C)
        q = q_ref[rows, :]
        k = k_ref[rows, :]
        v = v_ref[rows, :]
        s = lax.dot_general(q, k, _NT, preferred_element_type=F32) * dec
        o = jnp.dot(s.astype(BF16), v, preferred_element_type=F32)
        r = r_ref[...]
        o = o + jnp.dot(q, r.astype(BF16), preferred_element_type=F32) * q_dec
        kd = (k.astype(F32) * k_dec).astype(BF16)
        kv = lax.dot_general(kd, v, _TN, preferred_element_type=F32)
        r_ref[...] = chunk_dec * r + kv
        mu = jnp.mean(o, axis=-1, keepdims=True)
        yc = o - mu
        var = jnp.mean(yc * yc, axis=-1, keepdims=True)
        y = yc * lax.rsqrt(var + NORM_EPS)
        o_ref[rows, :] = (y * g_ref[rows, :].astype(F32)).astype(BF16)
        return carry

    lax.fori_loop(0, n_chunks, body, 0)


def _retention(proj, log_g):
    B, S, _ = proj.shape
    qb, kb = OFF_RQ // RET_DK, OFF_RK // RET_DK
    vb, gb = OFF_RV // RET_DV, OFF_RG // RET_DV
    return pl.pallas_call(
        _retention_kernel,
        out_shape=jax.ShapeDtypeStruct((B, S, RET_V), BF16),
        grid=(B, RET_HEADS),
        in_specs=[
            pl.BlockSpec((None, 1, LANES), lambda b, h: (h, 0, 0)),
            pl.BlockSpec((None, S, RET_DK), lambda b, h: (b, 0, qb + h)),
            pl.BlockSpec((None, S, RET_DK), lambda b, h: (b, 0, kb + h)),
            pl.BlockSpec((None, S, RET_DV), lambda b, h: (b, 0, vb + h)),
            pl.BlockSpec((None, S, RET_DV), lambda b, h: (b, 0, gb + h)),
        ],
        out_specs=pl.BlockSpec((None, S, RET_DV), lambda b, h: (b, 0, h)),
        scratch_shapes=[pltpu.VMEM((RET_DK, RET_DV), F32)],
        compiler_params=pltpu.CompilerParams(
            dimension_semantics=("parallel", "parallel"), vmem_limit_bytes=VMEM_LIMIT),
        name="retention",
    )(log_g, proj, proj, proj, proj)


SB_PAIRS = SB_W // LANES


def _sb_kernel(q_ref, k_ref, v_ref, t_ref, o_ref, carry_ref, acc_ref):
    BLK = SB_BLOCK
    n_blocks = q_ref.shape[0] // BLK
    lane = lax.broadcasted_iota(jnp.int32, (BLK, LANES), 1)
    head0 = lane < SB_DH
    qi_loc = lax.broadcasted_iota(jnp.int32, (2 * BLK, BLK), 0) % BLK
    ki_loc = lax.broadcasted_iota(jnp.int32, (2 * BLK, BLK), 1)
    diag_mask = ki_loc < qi_loc
    sign_bit = jnp.int32(-2 ** 31)

    def split_heads(x):
        zero = jnp.zeros_like(x)
        return jnp.concatenate([jnp.where(head0, x, zero), jnp.where(head0, zero, x)], axis=0)

    def tile(q2s, rows_k, first):
        pairs = range(SB_PAIRS)
        cols = [slice(p * LANES, (p + 1) * LANES) for p in pairs]
        zs = [lax.dot_general(q2s[p], k_ref[rows_k, cols[p]], _NT, preferred_element_type=F32)
              for p in pairs]
        sp_split = []
        for p in pairs:
            z = zs[p]
            neg_abs = pltpu.bitcast(pltpu.bitcast(z, jnp.int32) | sign_bit, F32)
            sp = jnp.maximum(z, 0.0) + jnp.log(1.0 + jnp.exp(neg_abs))
            if first:
                sp = jnp.where(diag_mask, sp, 0.0)
            sp_split.append(sp.astype(BF16))
        css = [jnp.dot(sp_split[p], t_ref[...], preferred_element_type=F32)
               for p in pairs]
        a2s = []
        for p in pairs:
            incl, tot = css[p][:, :BLK], css[p][:, BLK:]
            if first:
                a = jnp.where(diag_mask, jnp.exp(zs[p] - incl), 0.0)
                carry_ref[p] = tot
            else:
                carry = carry_ref[p]
                a = jnp.exp(zs[p] - incl - carry)
                carry_ref[p] = carry + tot
            ab = a.astype(BF16)
            a2s.append(jnp.concatenate([ab[:BLK], ab[BLK:]], axis=1))
        for p in pairs:
            pv = jnp.dot(a2s[p], split_heads(v_ref[rows_k, cols[p]]), preferred_element_type=F32)
            if first:
                acc_ref[p] = pv
            else:
                acc_ref[p] += pv

    def q_block(qi, _):
        rows_q = pl.ds(pl.multiple_of(qi * BLK, BLK), BLK)
        q2s = [split_heads(q_ref[rows_q, p * LANES:(p + 1) * LANES]) for p in range(SB_PAIRS)]
        tile(q2s, rows_q, True)

        def k_block(t, _):
            rows_k = pl.ds(pl.multiple_of((qi - 1 - t) * BLK, BLK), BLK)
            tile(q2s, rows_k, False)
            return 0

        lax.fori_loop(0, qi, k_block, 0)
        for p in range(SB_PAIRS):
            o_ref[rows_q, p * LANES:(p + 1) * LANES] = acc_ref[p].astype(BF16)
        return 0

    lax.fori_loop(0, n_blocks, q_block, 0)


def _stick_breaking(proj, tri):
    B, S, _ = proj.shape
    qb, kb, vb = OFF_SQ // SB_W, OFF_SK // SB_W, OFF_SV // SB_W
    return pl.pallas_call(
        _sb_kernel,
        out_shape=jax.ShapeDtypeStruct((B, S, SB_W), BF16),
        grid=(B,),
        in_specs=[
            pl.BlockSpec((None, S, SB_W), lambda b: (b, 0, qb)),
            pl.BlockSpec((None, S, SB_W), lambda b: (b, 0, kb)),
            pl.BlockSpec((None, S, SB_W), lambda b: (b, 0, vb)),
            pl.BlockSpec((SB_BLOCK, 2 * SB_BLOCK), lambda b: (0, 0)),
        ],
        out_specs=pl.BlockSpec((None, S, SB_W), lambda b: (b, 0, 0)),
        scratch_shapes=[pltpu.VMEM((SB_PAIRS, 2 * SB_BLOCK, SB_BLOCK), F32),
                        pltpu.VMEM((SB_PAIRS, SB_BLOCK, LANES), F32)],
        compiler_params=pltpu.CompilerParams(
            dimension_semantics=("parallel",), vmem_limit_bytes=VMEM_LIMIT),
        name="stick_breaking",
    )(proj, proj, proj, tri)


def _merge_kernel(h_ref, ret_ref, sb_ref, gr0_ref, gr1_ref, gs0_ref, gs1_ref,
                  wr_ref, ws_ref, wo_ref, o_ref):
    ro = jnp.dot(ret_ref[...], wr_ref[...], preferred_element_type=F32)
    so = jnp.dot(sb_ref[...], ws_ref[...], preferred_element_type=F32)
    gr = jnp.concatenate([gr0_ref[...], gr1_ref[...]], axis=1).astype(F32)
    gs = jnp.concatenate([gs0_ref[...], gs1_ref[...]], axis=1).astype(F32)
    merged = (gr * ro + gs * so).astype(BF16)
    o_ref[...] = h_ref[...] + jnp.dot(merged, wo_ref[...], preferred_element_type=F32)


def _merge(h, ret, sb, proj, w_ret_o, w_sb_o, w_out):
    B, S, D = h.shape
    tm = min(MERGE_TM, S)
    half = D_MODEL // 2
    gr0, gs0 = OFF_GR // half, OFF_GS // half
    tok = lambda width, col: pl.BlockSpec((None, tm, width), lambda b, i: (b, i, col))
    full = lambda a: pl.BlockSpec(a.shape, lambda b, i: (0, 0))
    return pl.pallas_call(
        _merge_kernel,
        out_shape=jax.ShapeDtypeStruct((B, S, D), F32),
        grid=(B, S // tm),
        in_specs=[tok(D, 0), tok(RET_V, 0), tok(SB_W, 0),
                  tok(half, gr0), tok(half, gr0 + 1), tok(half, gs0), tok(half, gs0 + 1),
                  full(w_ret_o), full(w_sb_o), full(w_out)],
        out_specs=tok(D, 0),
        compiler_params=pltpu.CompilerParams(
            dimension_semantics=("parallel", "parallel"), vmem_limit_bytes=VMEM_LIMIT),
        name="merge",
    )(h, ret, sb, proj, proj, proj, proj, w_ret_o, w_sb_o, w_out)


def _ffn_kernel(h_ref, g_ref, wg_ref, wu_ref, wd_ref, gf_ref, o_ref, hn_ref, *, final):
    f = pl.program_id(1)

    @pl.when(f == 0)
    def _():
        x = h_ref[...]
        hn_ref[...] = _rmsnorm_rows(x, g_ref[...]).astype(BF16)
        o_ref[...] = x

    hn = hn_ref[...]
    g = jnp.dot(hn, wg_ref[...], preferred_element_type=F32)
    u = jnp.dot(hn, wu_ref[...], preferred_element_type=F32)
    act = (g * _sigmoid(g) * u).astype(BF16)
    o_ref[...] += jnp.dot(act, wd_ref[...], preferred_element_type=F32)

    if final:
        @pl.when(f == pl.num_programs(1) - 1)
        def _():
            o_ref[...] = _rmsnorm_rows(o_ref[...], gf_ref[...])


def _ffn(h, g, w_gate_up, w_down, g_final, final):
    B, S, D = h.shape
    T = B * S
    tm = min(FFN_TM, T)
    h2 = h.reshape(T, D)
    nf = FF // FFN_TF
    out = pl.pallas_call(
        functools.partial(_ffn_kernel, final=final),
        out_shape=jax.ShapeDtypeStruct((T, D), F32),
        grid=(T // tm, nf),
        in_specs=[
            pl.BlockSpec((tm, D), lambda i, f: (i, 0)),
            pl.BlockSpec((1, D), lambda i, f: (0, 0)),
            pl.BlockSpec((D, FFN_TF), lambda i, f: (0, f)),
            pl.BlockSpec((D, FFN_TF), lambda i, f: (0, nf + f)),
            pl.BlockSpec((FFN_TF, D), lambda i, f: (f, 0)),
            pl.BlockSpec((1, D), lambda i, f: (0, 0)),
        ],
        out_specs=pl.BlockSpec((tm, D), lambda i, f: (i, 0)),
        scratch_shapes=[pltpu.VMEM((tm, D), BF16)],
        compiler_params=pltpu.CompilerParams(
            dimension_semantics=("parallel", "arbitrary"), vmem_limit_bytes=VMEM_LIMIT),
        name="ffn_final" if final else "ffn",
    )(h2, g, w_gate_up, w_gate_up, w_down, g_final)
    return out.reshape(B, S, D)


def _rope_tables(S):
    half = RET_DK // 2
    pos = jnp.arange(S, dtype=F32)
    inv = 1.0 / (ROPE_BASE ** (jnp.arange(half, dtype=F32) / half))
    ang = pos[:, None] * inv[None, :]
    cos = jnp.cos(ang)
    sin = jnp.sin(ang)
    return jnp.concatenate([cos, cos], axis=-1), jnp.concatenate([-sin, sin], axis=-1)


def kernel(x, norm_mix, w_in, w_ret_o, w_sb_o, w_out, norm_ffn, w_gate_up, w_down, norm_final):
    B, S, D = x.shape
    depth = w_in.shape[0]
    cos, sin = _rope_tables(S)
    log_g = jnp.log1p(-jnp.exp2(-5.0 - jnp.arange(RET_HEADS, dtype=F32)))
    log_g = jnp.broadcast_to(log_g[:, None, None], (RET_HEADS, 1, LANES))
    r = jnp.arange(SB_BLOCK)[:, None]
    c = jnp.arange(2 * SB_BLOCK)[None, :]
    tri = ((r >= c) | (c >= SB_BLOCK)).astype(BF16)
    g_final = norm_final.reshape(1, D)

    h = x
    for layer in range(depth):
        proj = _inproj(h, norm_mix[layer].reshape(1, D), w_in[layer].astype(BF16), cos, sin)
        ret = _retention(proj, log_g)
        sb = _stick_breaking(proj, tri)
        h = _merge(h, ret, sb, proj, w_ret_o[layer].astype(BF16), w_sb_o[layer].astype(BF16),
                   w_out[layer].astype(BF16))
        h = _ffn(h, norm_ffn[layer].reshape(1, D), w_gate_up[layer].astype(BF16),
                 w_down[layer].astype(BF16), g_final, final=(layer == depth - 1))
    return h
```

```python
import functools

import jax
import jax.numpy as jnp
from jax import lax
from jax.experimental import pallas as pl
from jax.experimental.pallas import tpu as pltpu

F32 = jnp.float32
BF16 = jnp.bfloat16

D_MODEL = 1024
RET_HEADS = 4
RET_DK = 128
RET_DV = 256
RET_QK = RET_HEADS * RET_DK
RET_V = RET_HEADS * RET_DV
RET_CHUNK = 128
ROPE_BASE = 10000.0
SB_HEADS = 8
SB_DH = 64
SB_W = SB_HEADS * SB_DH
SB_BLOCK = 128
FF = 2816
NORM_EPS = 1e-6
IN_COLS = 2 * RET_QK + 2 * RET_V + 3 * SB_W + 2 * D_MODEL

OFF_RQ = 0
OFF_RK = OFF_RQ + RET_QK
OFF_RV = OFF_RK + RET_QK
OFF_RG = OFF_RV + RET_V
OFF_SQ = OFF_RG + RET_V
OFF_SK = OFF_SQ + SB_W
OFF_SV = OFF_SK + SB_W
OFF_GR = OFF_SV + SB_W
OFF_GS = OFF_GR + D_MODEL

LANES = 128
PROJ_TN = 512
PROJ_ROWS = 256
FFN_TM = 2048
FFN_TF = 256
FFN_ROWS = 256
MERGE_TM = 1024
MERGE_ROWS = 256
RET_UNROLL = 8
VMEM_LIMIT = 56 * 1024 * 1024

_NT = (((1,), (1,)), ((), ()))
_TN = (((0,), (0,)), ((), ()))


def _sigmoid(x):
    return 1.0 / (1.0 + jnp.exp(-x))


def _rmsnorm_rows(x, g):
    ms = jnp.mean(x * x, axis=-1, keepdims=True)
    return x * lax.rsqrt(ms + NORM_EPS) * g


def _inproj_kernel(h_ref, g_ref, w_ref, cos_ref, sin_ref, o_ref, hn_ref):
    j = pl.program_id(1)
    S = hn_ref.shape[0]

    @pl.when(j == 0)
    def _():
        hn_ref[...] = _rmsnorm_rows(h_ref[...], g_ref[...]).astype(BF16)

    def project(epilogue):
        for r in range(0, S, PROJ_ROWS):
            rows = slice(r, r + PROJ_ROWS)
            y = jnp.dot(hn_ref[rows, :], w_ref[...], preferred_element_type=F32)
            epilogue(y, rows)

    j_rk = OFF_RK // PROJ_TN
    j_rv = OFF_RV // PROJ_TN
    j_rg = OFF_RG // PROJ_TN
    j_sq = OFF_SQ // PROJ_TN
    j_sk = OFF_SK // PROJ_TN
    j_gr = OFF_GR // PROJ_TN

    @pl.when(j < j_rv)
    def _():
        scale = jnp.where(j == j_rk, RET_DK ** -0.5, 1.0).astype(F32)

        def rope(y, rows):
            cos = cos_ref[rows, :]
            sin = sin_ref[rows, :]
            for hd in range(PROJ_TN // RET_DK):
                sl = slice(hd * RET_DK, (hd + 1) * RET_DK)
                yh = y[:, sl]
                r = yh * cos + pltpu.roll(yh, RET_DK // 2, 1) * sin
                o_ref[rows, sl] = (r * scale).astype(BF16)

        project(rope)

    @pl.when(((j >= j_rv) & (j < j_rg)) | ((j >= j_sk) & (j < j_gr)))
    def _():
        def identity(y, rows):
            o_ref[rows, :] = y.astype(BF16)

        project(identity)

    @pl.when((j >= j_rg) & (j < j_sq))
    def _():
        def silu(y, rows):
            o_ref[rows, :] = (y * _sigmoid(y)).astype(BF16)

        project(silu)

    @pl.when(j == j_sq)
    def _():
        def scaled(y, rows):
            o_ref[rows, :] = (y * (SB_DH ** -0.5)).astype(BF16)

        project(scaled)

    @pl.when(j >= j_gr)
    def _():
        def gate(y, rows):
            o_ref[rows, :] = _sigmoid(y).astype(BF16)

        project(gate)


def _inproj(h, g, w_in, cos, sin):
    B, S, D = h.shape
    return pl.pallas_call(
        _inproj_kernel,
        out_shape=jax.ShapeDtypeStruct((B, S, IN_COLS), BF16),
        grid=(B, IN_COLS // PROJ_TN),
        in_specs=[
            pl.BlockSpec((None, S, D), lambda b, j: (b, 0, 0)),
            pl.BlockSpec((1, D), lambda b, j: (0, 0)),
            pl.BlockSpec((D, PROJ_TN), lambda b, j: (0, j)),
            pl.BlockSpec((S, RET_DK), lambda b, j: (0, 0)),
            pl.BlockSpec((S, RET_DK), lambda b, j: (0, 0)),
        ],
        out_specs=pl.BlockSpec((None, S, PROJ_TN), lambda b, j: (b, 0, j)),
        scratch_shapes=[pltpu.VMEM((S, D), BF16)],
        compiler_params=pltpu.CompilerParams(
            dimension_semantics=("parallel", "arbitrary"), vmem_limit_bytes=VMEM_LIMIT),
        name="inproj",
    )(h, g, w_in, cos, sin)


def _retention_kernel(lg_ref, q_ref, k_ref, v_ref, g_ref, o_ref, kv_ref, r_ref):
    C = RET_CHUNK
    n_chunks = q_ref.shape[0] // C
    lg_row = lg_ref[...]
    lg = lg_row[:, :1]
    ii = lax.broadcasted_iota(jnp.int32, (C, C), 0)
    jj = lax.broadcasted_iota(jnp.int32, (C, C), 1)
    diff = (ii - jj).astype(F32)
    dec = jnp.where(diff >= 0, jnp.exp(lg_row * jnp.maximum(diff, 0.0)), 0.0)
    i_col = lax.broadcasted_iota(jnp.int32, (C, 1), 0).astype(F32)
    k_dec = jnp.exp(lg * (C - 1.0 - i_col))
    q_dec = jnp.exp(lg * (i_col + 1.0))
    chunk_dec = jnp.exp(lg * C)

    def chunk_rows(n):
        return pl.ds(pl.multiple_of(n * C, C), C)

    def kv_body(n, carry):
        rows = chunk_rows(n)
        kd = (k_ref[rows, :].astype(F32) * k_dec).astype(BF16)
        kv_ref[n] = lax.dot_general(kd, v_ref[rows, :], _TN, preferred_element_type=F32)
        return carry

    lax.fori_loop(0, n_chunks, kv_body, 0, unroll=RET_UNROLL)

    def scan_body(n, r):
        r_ref[n] = r.astype(BF16)
        return chunk_dec * r + kv_ref[n]

    lax.fori_loop(0, n_chunks, scan_body, jnp.zeros((RET_DK, RET_DV), F32))

    def out_body(n, carry):
        rows = chunk_rows(n)
        q = q_ref[rows, :]
        v = v_ref[rows, :]
        s = lax.dot_general(q, k_ref[rows, :], _NT, preferred_element_type=F32) * dec
        lhs = jnp.concatenate([s.astype(BF16), (q.astype(F32) * q_dec).astype(BF16)], axis=1)
        rhs = jnp.concatenate([v, r_ref[n]], axis=0)
        o = jnp.dot(lhs, rhs, preferred_element_type=F32)
        mu = jnp.mean(o, axis=-1, keepdims=True)
        yc = o - mu
        var = jnp.mean(yc * yc, axis=-1, keepdims=True)
        y = yc * lax.rsqrt(var + NORM_EPS)
        o_ref[rows, :] = (y * g_ref[rows, :].astype(F32)).astype(BF16)
        return carry

    lax.fori_loop(0, n_chunks, out_body, 0, unroll=RET_UNROLL)


def _retention(proj, log_g):
    B, S, _ = proj.shape
    qb, kb = OFF_RQ // RET_DK, OFF_RK // RET_DK
    vb, gb = OFF_RV // RET_DV, OFF_RG // RET_DV
    return pl.pallas_call(
        _retention_kernel,
        out_shape=jax.ShapeDtypeStruct((B, S, RET_V), BF16),
        grid=(B, RET_HEADS),
        in_specs=[
            pl.BlockSpec((None, 1, LANES), lambda b, h: (h, 0, 0)),
            pl.BlockSpec((None, S, RET_DK), lambda b, h: (b, 0, qb + h)),
            pl.BlockSpec((None, S, RET_DK), lambda b, h: (b, 0, kb + h)),
            pl.BlockSpec((None, S, RET_DV), lambda b, h: (b, 0, vb + h)),
            pl.BlockSpec((None, S, RET_DV), lambda b, h: (b, 0, gb + h)),
        ],
        out_specs=pl.BlockSpec((None, S, RET_DV), lambda b, h: (b, 0, h)),
        scratch_shapes=[pltpu.VMEM((S // RET_CHUNK, RET_DK, RET_DV), F32),
                        pltpu.VMEM((S // RET_CHUNK, RET_DK, RET_DV), BF16)],
        compiler_params=pltpu.CompilerParams(
            dimension_semantics=("parallel", "parallel"), vmem_limit_bytes=VMEM_LIMIT),
        name="retention",
    )(log_g, proj, proj, proj, proj)


SB_PAIRS = SB_W // LANES


def _sb_kernel(q_ref, k_ref, v_ref, t_ref, o_ref, carry_ref, acc_ref, z_ref, a_ref):
    BLK = SB_BLOCK
    n_blocks = q_ref.shape[0] // BLK
    pairs = range(SB_PAIRS)
    cols = [slice(p * LANES, (p + 1) * LANES) for p in pairs]
    lane = lax.broadcasted_iota(jnp.int32, (BLK, LANES), 1)
    head0 = lane < SB_DH
    qi_loc = lax.broadcasted_iota(jnp.int32, (2 * BLK, BLK), 0) % BLK
    ki_loc = lax.broadcasted_iota(jnp.int32, (2 * BLK, BLK), 1)
    diag_mask = ki_loc < qi_loc
    sign_bit = jnp.int32(-2 ** 31)

    def block_rows(j):
        return pl.ds(pl.multiple_of(j * BLK, BLK), BLK)

    def split_heads(x):
        zero = jnp.zeros_like(x)
        return jnp.concatenate([jnp.where(head0, x, zero), jnp.where(head0, zero, x)], axis=0)

    def scores(q2s, j, slot):
        rows_k = block_rows(j)
        for p in pairs:
            z_ref[slot, p] = lax.dot_general(q2s[p], k_ref[rows_k, cols[p]], _NT,
                                             preferred_element_type=F32)

    def weights(zs, first):
        sps = []
        for p in pairs:
            z = zs[p]
            neg_abs = pltpu.bitcast(pltpu.bitcast(z, jnp.int32) | sign_bit, F32)
            sp = jnp.maximum(z, 0.0) + jnp.log(1.0 + jnp.exp(neg_abs))
            if first:
                sp = jnp.where(diag_mask, sp, 0.0)
            sps.append(sp.astype(BF16))
        css = [jnp.dot(sps[p], t_ref[...], preferred_element_type=F32)
               for p in pairs]
        for p in pairs:
            if first:
                suffix = css[p]
                a = jnp.where(diag_mask, jnp.exp(zs[p] - suffix), 0.0)
            else:
                suffix = css[p] + carry_ref[p]
                a = jnp.exp(zs[p] - suffix)
            carry_ref[p] = jnp.broadcast_to(suffix[:, :1], suffix.shape)
            ab = a.astype(BF16)
            a_ref[p] = jnp.concatenate([ab[:BLK], ab[BLK:]], axis=1)

    def values(j):
        rows_k = block_rows(j)
        for p in pairs:
            acc_ref[p] += jnp.dot(a_ref[p], split_heads(v_ref[rows_k, cols[p]]),
                                  preferred_element_type=F32)

    def q_block(qi, _):
        rows_q = block_rows(qi)
        q2s = [split_heads(q_ref[rows_q, cols[p]]) for p in pairs]
        for p in pairs:
            acc_ref[p] = jnp.zeros((BLK, LANES), F32)
        scores(q2s, qi, 0)
        scores(q2s, jnp.maximum(qi - 1, 0), 1)
        weights([z_ref[0, p] for p in pairs], True)

        def step(s, _):
            slot = lax.rem(s, 2)
            zs = [z_ref[slot, p] for p in pairs]
            values(qi - s + 1)
            scores(q2s, jnp.maximum(qi - s - 1, 0), 1 - slot)
            weights(zs, False)
            return 0

        lax.fori_loop(1, qi + 1, step, 0)
        values(0)
        for p in pairs:
            o_ref[rows_q, cols[p]] = acc_ref[p].astype(BF16)
        return 0

    lax.fori_loop(0, n_blocks, q_block, 0)


def _stick_breaking(proj, tri):
    B, S, _ = proj.shape
    qb, kb, vb = OFF_SQ // SB_W, OFF_SK // SB_W, OFF_SV // SB_W
    return pl.pallas_call(
        _sb_kernel,
        out_shape=jax.ShapeDtypeStruct((B, S, SB_W), BF16),
        grid=(B,),
        in_specs=[
            pl.BlockSpec((None, S, SB_W), lambda b: (b, 0, qb)),
            pl.BlockSpec((None, S, SB_W), lambda b: (b, 0, kb)),
            pl.BlockSpec((None, S, SB_W), lambda b: (b, 0, vb)),
            pl.BlockSpec((SB_BLOCK, SB_BLOCK), lambda b: (0, 0)),
        ],
        out_specs=pl.BlockSpec((None, S, SB_W), lambda b: (b, 0, 0)),
        scratch_shapes=[pltpu.VMEM((SB_PAIRS, 2 * SB_BLOCK, SB_BLOCK), F32),
                        pltpu.VMEM((SB_PAIRS, SB_BLOCK, LANES), F32),
                        pltpu.VMEM((2, SB_PAIRS, 2 * SB_BLOCK, SB_BLOCK), F32),
                        pltpu.VMEM((SB_PAIRS, SB_BLOCK, 2 * SB_BLOCK), BF16)],
        compiler_params=pltpu.CompilerParams(
            dimension_semantics=("parallel",), vmem_limit_bytes=VMEM_LIMIT),
        name="stick_breaking",
    )(proj, proj, proj, tri)


def _merge_kernel(h_ref, ret_ref, sb_ref, gr0_ref, gr1_ref, gs0_ref, gs1_ref,
                  wr_ref, ws_ref, wo_ref, o_ref):
    def branches(rows):
        ro = jnp.dot(ret_ref[rows, :], wr_ref[...], preferred_element_type=F32)
        so = jnp.dot(sb_ref[rows, :], ws_ref[...], preferred_element_type=F32)
        gr = jnp.concatenate([gr0_ref[rows, :], gr1_ref[rows, :]], axis=1).astype(F32)
        gs = jnp.concatenate([gs0_ref[rows, :], gs1_ref[rows, :]], axis=1).astype(F32)
        return (gr * ro + gs * so).astype(BF16)

    def project(rows, merged):
        o_ref[rows, :] = h_ref[rows, :] + jnp.dot(merged, wo_ref[...], preferred_element_type=F32)

    chunks = [slice(r, r + MERGE_ROWS) for r in range(0, o_ref.shape[0], MERGE_ROWS)]
    prev = None
    for rows in chunks:
        merged = branches(rows)
        if prev is not None:
            project(*prev)
        prev = (rows, merged)
    project(*prev)


def _merge(h, ret, sb, proj, w_ret_o, w_sb_o, w_out):
    B, S, D = h.shape
    tm = min(MERGE_TM, S)
    half = D_MODEL // 2
    gr0, gs0 = OFF_GR // half, OFF_GS // half
    tok = lambda width, col: pl.BlockSpec((None, tm, width), lambda b, i: (b, i, col))
    full = lambda a: pl.BlockSpec(a.shape, lambda b, i: (0, 0))
    return pl.pallas_call(
        _merge_kernel,
        out_shape=jax.ShapeDtypeStruct((B, S, D), F32),
        grid=(B, S // tm),
        in_specs=[tok(D, 0), tok(RET_V, 0), tok(SB_W, 0),
                  tok(half, gr0), tok(half, gr0 + 1), tok(half, gs0), tok(half, gs0 + 1),
                  full(w_ret_o), full(w_sb_o), full(w_out)],
        out_specs=tok(D, 0),
        compiler_params=pltpu.CompilerParams(
            dimension_semantics=("parallel", "parallel"), vmem_limit_bytes=VMEM_LIMIT),
        name="merge",
    )(h, ret, sb, proj, proj, proj, proj, w_ret_o, w_sb_o, w_out)


def _ffn_kernel(h_ref, g_ref, wg_ref, wu_ref, wd_ref, gf_ref, o_ref, hn_ref, *, final):
    f = pl.program_id(1)

    @pl.when(f == 0)
    def _():
        x = h_ref[...]
        hn_ref[...] = _rmsnorm_rows(x, g_ref[...]).astype(BF16)
        o_ref[...] = x

    def up(rows):
        hn = hn_ref[rows, :]
        g = jnp.dot(hn, wg_ref[...], preferred_element_type=F32)
        u = jnp.dot(hn, wu_ref[...], preferred_element_type=F32)
        return (g * _sigmoid(g) * u).astype(BF16)

    def down(rows, act):
        o_ref[rows, :] += jnp.dot(act, wd_ref[...], preferred_element_type=F32)

    chunks = [slice(r, r + FFN_ROWS) for r in range(0, hn_ref.shape[0], FFN_ROWS)]
    prev = None
    for rows in chunks:
        act = up(rows)
        if prev is not None:
            down(*prev)
        prev = (rows, act)
    down(*prev)

    if final:
        @pl.when(f == pl.num_programs(1) - 1)
        def _():
            o_ref[...] = _rmsnorm_rows(o_ref[...], gf_ref[...])


def _ffn(h, g, w_gate_up, w_down, g_final, final):
    B, S, D = h.shape
    T = B * S
    tm = min(FFN_TM, T)
    h2 = h.reshape(T, D)
    nf = FF // FFN_TF
    out = pl.pallas_call(
        functools.partial(_ffn_kernel, final=final),
        out_shape=jax.ShapeDtypeStruct((T, D), F32),
        grid=(T // tm, nf),
        in_specs=[
            pl.BlockSpec((tm, D), lambda i, f: (i, 0)),
            pl.BlockSpec((1, D), lambda i, f: (0, 0)),
            pl.BlockSpec((D, FFN_TF), lambda i, f: (0, f)),
            pl.BlockSpec((D, FFN_TF), lambda i, f: (0, nf + f)),
            pl.BlockSpec((FFN_TF, D), lambda i, f: (f, 0)),
            pl.BlockSpec((1, D), lambda i, f: (0, 0)),
        ],
        out_specs=pl.BlockSpec((tm, D), lambda i, f: (i, 0)),
        scratch_shapes=[pltpu.VMEM((tm, D), BF16)],
        compiler_params=pltpu.CompilerParams(
            dimension_semantics=("parallel", "arbitrary"), vmem_limit_bytes=VMEM_LIMIT),
        name="ffn_final" if final else "ffn",
    )(h2, g, w_gate_up, w_gate_up, w_down, g_final)
    return out.reshape(B, S, D)


def _rope_tables(S):
    half = RET_DK // 2
    pos = jnp.arange(S, dtype=F32)
    inv = 1.0 / (ROPE_BASE ** (jnp.arange(half, dtype=F32) / half))
    ang = pos[:, None] * inv[None, :]
    cos = jnp.cos(ang)
    sin = jnp.sin(ang)
    return jnp.concatenate([cos, cos], axis=-1), jnp.concatenate([-sin, sin], axis=-1)


def kernel(x, norm_mix, w_in, w_ret_o, w_sb_o, w_out, norm_ffn, w_gate_up, w_down, norm_final):
    B, S, D = x.shape
    depth = w_in.shape[0]
    cos, sin = _rope_tables(S)
    log_g = jnp.log1p(-jnp.exp2(-5.0 - jnp.arange(RET_HEADS, dtype=F32)))
    log_g = jnp.broadcast_to(log_g[:, None, None], (RET_HEADS, 1, LANES))
    r = jnp.arange(SB_BLOCK)[:, None]
    c = jnp.arange(SB_BLOCK)[None, :]
    tri = (r >= c).astype(BF16)
    g_final = norm_final.reshape(1, D)

    h = x
    for layer in range(depth):
        proj = _inproj(h, norm_mix[layer].reshape(1, D), w_in[layer].astype(BF16), cos, sin)
        ret = _retention(proj, log_g)
        sb = _stick_breaking(proj, tri)
        h = _merge(h, ret, sb, proj, w_ret_o[layer].astype(BF16), w_sb_o[layer].astype(BF16),
                   w_out[layer].astype(BF16))
        h = _ffn(h, norm_ffn[layer].reshape(1, D), w_gate_up[layer].astype(BF16),
                 w_down[layer].astype(BF16), g_final, final=(layer == depth - 1))
    return h
```

```python
import functools

import numpy as np
import jax
import jax.numpy as jnp
from jax import lax
from jax.experimental import pallas as pl
from jax.experimental.pallas import tpu as pltpu

F32 = jnp.float32
BF16 = jnp.bfloat16

D_MODEL = 1024
RET_HEADS = 4
RET_DK = 128
RET_DV = 256
RET_QK = RET_HEADS * RET_DK
RET_V = RET_HEADS * RET_DV
RET_CHUNK = 128
ROPE_BASE = 10000.0
SB_HEADS = 8
SB_DH = 64
SB_W = SB_HEADS * SB_DH
SB_BLOCK = 128
FF = 2816
NORM_EPS = 1e-6
IN_COLS = 2 * RET_QK + 2 * RET_V + 3 * SB_W + 2 * D_MODEL

OFF_RQ = 0
OFF_RK = OFF_RQ + RET_QK
OFF_RV = OFF_RK + RET_QK
OFF_RG = OFF_RV + RET_V
OFF_SQ = OFF_RG + RET_V
OFF_SK = OFF_SQ + SB_W
OFF_SV = OFF_SK + SB_W
OFF_GR = OFF_SV + SB_W
OFF_GS = OFF_GR + D_MODEL

LANES = 128
PROJ_TN = 512
PROJ_ROWS = 256
FFN_TM = 2048
FFN_TF = 256
FFN_ROWS = 256
MERGE_TM = 1024
MERGE_ROWS = 256
RET_UNROLL = 8
VMEM_LIMIT = 56 * 1024 * 1024

_NT = (((1,), (1,)), ((), ()))
_TN = (((0,), (0,)), ((), ()))


def _sigmoid(x):
    return 1.0 / (1.0 + jnp.exp(-x))


def _rmsnorm_rows(x, g):
    ms = jnp.mean(x * x, axis=-1, keepdims=True)
    return x * lax.rsqrt(ms + NORM_EPS) * g


def _inproj_kernel(h_ref, g_ref, w_ref, cos_ref, sin_ref, o_ref, hn_ref):
    j = pl.program_id(1)
    S = hn_ref.shape[0]

    @pl.when(j == 0)
    def _():
        hn_ref[...] = _rmsnorm_rows(h_ref[...], g_ref[...]).astype(BF16)

    def project(epilogue):
        for r in range(0, S, PROJ_ROWS):
            rows = slice(r, r + PROJ_ROWS)
            y = jnp.dot(hn_ref[rows, :], w_ref[...], preferred_element_type=F32)
            epilogue(y, rows)

    j_rk = OFF_RK // PROJ_TN
    j_rv = OFF_RV // PROJ_TN
    j_rg = OFF_RG // PROJ_TN
    j_sq = OFF_SQ // PROJ_TN
    j_sk = OFF_SK // PROJ_TN
    j_gr = OFF_GR // PROJ_TN

    @pl.when(j < j_rv)
    def _():
        scale = jnp.where(j == j_rk, RET_DK ** -0.5, 1.0).astype(F32)

        def rope(y, rows):
            cos = cos_ref[rows, :]
            sin = sin_ref[rows, :]
            for hd in range(PROJ_TN // RET_DK):
                sl = slice(hd * RET_DK, (hd + 1) * RET_DK)
                yh = y[:, sl]
                r = yh * cos + pltpu.roll(yh, RET_DK // 2, 1) * sin
                o_ref[rows, sl] = (r * scale).astype(BF16)

        project(rope)

    @pl.when(((j >= j_rv) & (j < j_rg)) | ((j >= j_sk) & (j < j_gr)))
    def _():
        def identity(y, rows):
            o_ref[rows, :] = y.astype(BF16)

        project(identity)

    @pl.when((j >= j_rg) & (j < j_sq))
    def _():
        def silu(y, rows):
            o_ref[rows, :] = (y * _sigmoid(y)).astype(BF16)

        project(silu)

    @pl.when(j == j_sq)
    def _():
        def scaled(y, rows):
            o_ref[rows, :] = (y * (SB_DH ** -0.5)).astype(BF16)

        project(scaled)

    @pl.when(j >= j_gr)
    def _():
        def gate(y, rows):
            o_ref[rows, :] = _sigmoid(y).astype(BF16)

        project(gate)


def _inproj(h, g, w_in, cos, sin):
    B, S, D = h.shape
    return pl.pallas_call(
        _inproj_kernel,
        out_shape=jax.ShapeDtypeStruct((B, S, IN_COLS), BF16),
        grid=(B, IN_COLS // PROJ_TN),
        in_specs=[
            pl.BlockSpec((None, S, D), lambda b, j: (b, 0, 0)),
            pl.BlockSpec((1, D), lambda b, j: (0, 0)),
            pl.BlockSpec((D, PROJ_TN), lambda b, j: (0, j)),
            pl.BlockSpec((S, RET_DK), lambda b, j: (0, 0)),
            pl.BlockSpec((S, RET_DK), lambda b, j: (0, 0)),
        ],
        out_specs=pl.BlockSpec((None, S, PROJ_TN), lambda b, j: (b, 0, j)),
        scratch_shapes=[pltpu.VMEM((S, D), BF16)],
        compiler_params=pltpu.CompilerParams(
            dimension_semantics=("parallel", "arbitrary"), vmem_limit_bytes=VMEM_LIMIT),
        name="inproj",
    )(h, g, w_in, cos, sin)


def _retention_kernel(lg_ref, q_ref, k_ref, v_ref, g_ref, o_ref, kv_ref, r_ref):
    C = RET_CHUNK
    n_chunks = q_ref.shape[0] // C
    lg_row = lg_ref[...]
    lg = lg_row[:, :1]
    ii = lax.broadcasted_iota(jnp.int32, (C, C), 0)
    jj = lax.broadcasted_iota(jnp.int32, (C, C), 1)
    diff = (ii - jj).astype(F32)
    dec = jnp.where(diff >= 0, jnp.exp(lg_row * jnp.maximum(diff, 0.0)), 0.0)
    i_col = lax.broadcasted_iota(jnp.int32, (C, 1), 0).astype(F32)
    k_dec = jnp.exp(lg * (C - 1.0 - i_col))
    q_dec = jnp.exp(lg * (i_col + 1.0))
    chunk_dec = jnp.exp(lg * C)

    def chunk_rows(n):
        return pl.ds(pl.multiple_of(n * C, C), C)

    def kv_body(n, carry):
        rows = chunk_rows(n)
        kd = (k_ref[rows, :].astype(F32) * k_dec).astype(BF16)
        kv_ref[n] = lax.dot_general(kd, v_ref[rows, :], _TN, preferred_element_type=F32)
        return carry

    lax.fori_loop(0, n_chunks, kv_body, 0, unroll=RET_UNROLL)

    def scan_body(n, r):
        r_ref[n] = r.astype(BF16)
        return chunk_dec * r + kv_ref[n]

    lax.fori_loop(0, n_chunks, scan_body, jnp.zeros((RET_DK, RET_DV), F32))

    def out_body(n, carry):
        rows = chunk_rows(n)
        q = q_ref[rows, :]
        v = v_ref[rows, :]
        s = lax.dot_general(q, k_ref[rows, :], _NT, preferred_element_type=F32) * dec
        lhs = jnp.concatenate([s.astype(BF16), (q.astype(F32) * q_dec).astype(BF16)], axis=1)
        rhs = jnp.concatenate([v, r_ref[n]], axis=0)
        o = jnp.dot(lhs, rhs, preferred_element_type=F32)
        mu = jnp.mean(o, axis=-1, keepdims=True)
        yc = o - mu
        var = jnp.mean(yc * yc, axis=-1, keepdims=True)
        y = yc * lax.rsqrt(var + NORM_EPS)
        o_ref[rows, :] = (y * g_ref[rows, :].astype(F32)).astype(BF16)
        return carry

    lax.fori_loop(0, n_chunks, out_body, 0, unroll=RET_UNROLL)


def _retention(proj, log_g):
    B, S, _ = proj.shape
    qb, kb = OFF_RQ // RET_DK, OFF_RK // RET_DK
    vb, gb = OFF_RV // RET_DV, OFF_RG // RET_DV
    return pl.pallas_call(
        _retention_kernel,
        out_shape=jax.ShapeDtypeStruct((B, S, RET_V), BF16),
        grid=(B, RET_HEADS),
        in_specs=[
            pl.BlockSpec((None, 1, LANES), lambda b, h: (h, 0, 0)),
            pl.BlockSpec((None, S, RET_DK), lambda b, h: (b, 0, qb + h)),
            pl.BlockSpec((None, S, RET_DK), lambda b, h: (b, 0, kb + h)),
            pl.BlockSpec((None, S, RET_DV), lambda b, h: (b, 0, vb + h)),
            pl.BlockSpec((None, S, RET_DV), lambda b, h: (b, 0, gb + h)),
        ],
        out_specs=pl.BlockSpec((None, S, RET_DV), lambda b, h: (b, 0, h)),
        scratch_shapes=[pltpu.VMEM((S // RET_CHUNK, RET_DK, RET_DV), F32),
                        pltpu.VMEM((S // RET_CHUNK, RET_DK, RET_DV), BF16)],
        compiler_params=pltpu.CompilerParams(
            dimension_semantics=("parallel", "parallel"), vmem_limit_bytes=VMEM_LIMIT),
        name="retention",
    )(log_g, proj, proj, proj, proj)


SB_PAIRS = SB_W // LANES
SB_STAGES = 5
SB_Z_SLOTS = 4
SB_UNROLL = 4
SB_DIAG_GROUP = 4


def _sb_schedule(n_blocks):
    tiles = [(qi, qi - s) for qi in range(1, n_blocks) for s in range(1, qi + 1)]
    pad = SB_STAGES - 1
    n_steps = -(-(len(tiles) + pad) // SB_UNROLL) * SB_UNROLL
    tab = np.zeros((3, n_steps + pad), np.int32)
    tab[1, :] = n_blocks
    for t, (qi, j) in enumerate(tiles):
        tab[:, t + pad] = (qi, qi, j)
    return tab


def _sb_kernel(tab_ref, q_ref, k_ref, v_ref, t_ref, o_ref,
               q2_ref, v2_ref, carry_ref, acc_ref, z_ref, sp_ref, cs_ref, a_ref):
    BLK = SB_BLOCK
    n_blocks = q_ref.shape[0] // BLK
    pad = SB_STAGES - 1
    n_steps = tab_ref.shape[1] - pad
    pairs = range(SB_PAIRS)
    cols = [slice(p * LANES, (p + 1) * LANES) for p in pairs]
    lane = lax.broadcasted_iota(jnp.int32, (BLK, LANES), 1)
    head0 = lane < SB_DH
    qi_loc = lax.broadcasted_iota(jnp.int32, (2 * BLK, BLK), 0) % BLK
    ki_loc = lax.broadcasted_iota(jnp.int32, (2 * BLK, BLK), 1)
    diag_mask = ki_loc < qi_loc
    sign_bit = jnp.int32(-2 ** 31)

    def block_rows(j):
        return pl.ds(pl.multiple_of(j * BLK, BLK), BLK)

    def split_heads(x):
        zero = jnp.zeros_like(x)
        return jnp.concatenate([jnp.where(head0, x, zero), jnp.where(head0, zero, x)], axis=0)

    def softplus(z):
        neg_abs = pltpu.bitcast(pltpu.bitcast(z, jnp.int32) | sign_bit, F32)
        return jnp.maximum(z, 0.0) + jnp.log(1.0 + jnp.exp(neg_abs))

    def stack_heads(a):
        ab = a.astype(BF16)
        return jnp.concatenate([ab[:BLK], ab[BLK:]], axis=1)

    def lane_bcast_col0(x):
        return jnp.broadcast_to(x[:, :1], x.shape)

    def prep(b, _):
        rows = block_rows(b)
        for p in pairs:
            q2_ref[b, p] = split_heads(q_ref[rows, cols[p]])
            v2_ref[b, p] = split_heads(v_ref[rows, cols[p]])
        return 0

    lax.fori_loop(0, n_blocks, prep, 0)
    z_ref[...] = jnp.zeros_like(z_ref)
    sp_ref[...] = jnp.zeros_like(sp_ref)
    cs_ref[...] = jnp.zeros_like(cs_ref)
    a_ref[...] = jnp.zeros_like(a_ref)
    carry_ref[n_blocks] = jnp.zeros(carry_ref.shape[1:], F32)
    acc_ref[n_blocks] = jnp.zeros(acc_ref.shape[1:], F32)

    def diag(i, _):
        work = [(SB_DIAG_GROUP * i + d, p) for d in range(SB_DIAG_GROUP) for p in pairs]
        zs = [lax.dot_general(q2_ref[qi, p], k_ref[block_rows(qi), cols[p]], _NT,
                              preferred_element_type=F32) for qi, p in work]
        sps = [jnp.where(diag_mask, softplus(z), 0.0).astype(BF16) for z in zs]
        css = [jnp.dot(sp, t_ref[...], preferred_element_type=F32) for sp in sps]
        a2s = []
        for (qi, p), z, suffix in zip(work, zs, css):
            a2s.append(stack_heads(jnp.where(diag_mask, jnp.exp(z - suffix), 0.0)))
            carry_ref[qi, p] = lane_bcast_col0(suffix)
        for (qi, p), a2 in zip(work, a2s):
            acc_ref[qi, p] = jnp.dot(a2, v2_ref[qi, p], preferred_element_type=F32)
        return 0

    lax.fori_loop(0, n_blocks // SB_DIAG_GROUP, diag, 0)

    def step(s, u):
        qi_qk, j_qk = tab_ref[0, s + pad], tab_ref[2, s + pad]
        slot_exp = tab_ref[1, s + pad - 3]
        slot_av, j_av = tab_ref[1, s], tab_ref[2, s]
        zslot_qk = u % SB_Z_SLOTS
        zslot_sp = (u - 1) % SB_Z_SLOTS
        zslot_exp = (u - 3) % SB_Z_SLOTS
        rows_qk = block_rows(j_qk)
        pvs = [jnp.dot(a_ref[p], v2_ref[j_av, p], preferred_element_type=F32) for p in pairs]
        css = [jnp.dot(sp_ref[p], t_ref[...], preferred_element_type=F32) for p in pairs]
        zs = [lax.dot_general(q2_ref[qi_qk, p], k_ref[rows_qk, cols[p]], _NT,
                              preferred_element_type=F32) for p in pairs]
        sps = [softplus(z_ref[zslot_sp, p]).astype(BF16) for p in pairs]
        a2s, carries = [], []
        for p in pairs:
            suffix = cs_ref[p] + carry_ref[slot_exp, p]
            a2s.append(stack_heads(jnp.exp(z_ref[zslot_exp, p] - suffix)))
            carries.append(lane_bcast_col0(suffix))
        for p in pairs:
            sp_ref[p] = sps[p]
            a_ref[p] = a2s[p]
            carry_ref[slot_exp, p] = carries[p]
            cs_ref[p] = css[p]
            z_ref[zslot_qk, p] = zs[p]
            acc_ref[slot_av, p] += pvs[p]

    def steps(i, _):
        for u in range(SB_UNROLL):
            step(i * SB_UNROLL + u, u)
        return 0

    lax.fori_loop(0, n_steps // SB_UNROLL, steps, 0)

    def emit(b, _):
        rows = block_rows(b)
        for p in pairs:
            o_ref[rows, cols[p]] = acc_ref[b, p].astype(BF16)
        return 0

    lax.fori_loop(0, n_blocks, emit, 0)


def _stick_breaking(proj, tri):
    B, S, _ = proj.shape
    n_blocks = S // SB_BLOCK
    assert n_blocks % SB_DIAG_GROUP == 0
    qb, kb, vb = OFF_SQ // SB_W, OFF_SK // SB_W, OFF_SV // SB_W
    tab = jnp.asarray(_sb_schedule(n_blocks))
    stacked = (SB_PAIRS, 2 * SB_BLOCK, SB_BLOCK)
    grid_spec = pltpu.PrefetchScalarGridSpec(
        num_scalar_prefetch=1,
        grid=(B,),
        in_specs=[
            pl.BlockSpec((None, S, SB_W), lambda b, tab: (b, 0, qb)),
            pl.BlockSpec((None, S, SB_W), lambda b, tab: (b, 0, kb)),
            pl.BlockSpec((None, S, SB_W), lambda b, tab: (b, 0, vb)),
            pl.BlockSpec((SB_BLOCK, SB_BLOCK), lambda b, tab: (0, 0)),
        ],
        out_specs=pl.BlockSpec((None, S, SB_W), lambda b, tab: (b, 0, 0)),
        scratch_shapes=[
            pltpu.VMEM((n_blocks,) + stacked, BF16),
            pltpu.VMEM((n_blocks,) + stacked, BF16),
            pltpu.VMEM((n_blocks + 1,) + stacked, F32),
            pltpu.VMEM((n_blocks + 1, SB_PAIRS, SB_BLOCK, LANES), F32),
            pltpu.VMEM((SB_Z_SLOTS,) + stacked, F32),
            pltpu.VMEM(stacked, BF16),
            pltpu.VMEM(stacked, F32),
            pltpu.VMEM((SB_PAIRS, SB_BLOCK, 2 * SB_BLOCK), BF16),
        ],
    )
    return pl.pallas_call(
        _sb_kernel,
        out_shape=jax.ShapeDtypeStruct((B, S, SB_W), BF16),
        grid_spec=grid_spec,
        compiler_params=pltpu.CompilerParams(
            dimension_semantics=("parallel",), vmem_limit_bytes=VMEM_LIMIT),
        name="stick_breaking",
    )(tab, proj, proj, proj, tri)


def _merge_kernel(h_ref, ret_ref, sb_ref, gr0_ref, gr1_ref, gs0_ref, gs1_ref,
                  wr_ref, ws_ref, wo_ref, o_ref):
    def branches(rows):
        ro = jnp.dot(ret_ref[rows, :], wr_ref[...], preferred_element_type=F32)
        so = jnp.dot(sb_ref[rows, :], ws_ref[...], preferred_element_type=F32)
        gr = jnp.concatenate([gr0_ref[rows, :], gr1_ref[rows, :]], axis=1).astype(F32)
        gs = jnp.concatenate([gs0_ref[rows, :], gs1_ref[rows, :]], axis=1).astype(F32)
        return (gr * ro + gs * so).astype(BF16)

    def project(rows, merged):
        o_ref[rows, :] = h_ref[rows, :] + jnp.dot(merged, wo_ref[...], preferred_element_type=F32)

    chunks = [slice(r, r + MERGE_ROWS) for r in range(0, o_ref.shape[0], MERGE_ROWS)]
    prev = None
    for rows in chunks:
        merged = branches(rows)
        if prev is not None:
            project(*prev)
        prev = (rows, merged)
    project(*prev)


def _merge(h, ret, sb, proj, w_ret_o, w_sb_o, w_out):
    B, S, D = h.shape
    tm = min(MERGE_TM, S)
    half = D_MODEL // 2
    gr0, gs0 = OFF_GR // half, OFF_GS // half
    tok = lambda width, col: pl.BlockSpec((None, tm, width), lambda b, i: (b, i, col))
    full = lambda a: pl.BlockSpec(a.shape, lambda b, i: (0, 0))
    return pl.pallas_call(
        _merge_kernel,
        out_shape=jax.ShapeDtypeStruct((B, S, D), F32),
        grid=(B, S // tm),
        in_specs=[tok(D, 0), tok(RET_V, 0), tok(SB_W, 0),
                  tok(half, gr0), tok(half, gr0 + 1), tok(half, gs0), tok(half, gs0 + 1),
                  full(w_ret_o), full(w_sb_o), full(w_out)],
        out_specs=tok(D, 0),
        compiler_params=pltpu.CompilerParams(
            dimension_semantics=("parallel", "parallel"), vmem_limit_bytes=VMEM_LIMIT),
        name="merge",
    )(h, ret, sb, proj, proj, proj, proj, w_ret_o, w_sb_o, w_out)


def _ffn_kernel(h_ref, g_ref, wg_ref, wu_ref, wd_ref, gf_ref, o_ref, hn_ref, *, final):
    f = pl.program_id(1)

    @pl.when(f == 0)
    def _():
        x = h_ref[...]
        hn_ref[...] = _rmsnorm_rows(x, g_ref[...]).astype(BF16)
        o_ref[...] = x

    def up(rows):
        hn = hn_ref[rows, :]
        g = jnp.dot(hn, wg_ref[...], preferred_element_type=F32)
        u = jnp.dot(hn, wu_ref[...], preferred_element_type=F32)
        return (g * _sigmoid(g) * u).astype(BF16)

    def down(rows, act):
        o_ref[rows, :] += jnp.dot(act, wd_ref[...], preferred_element_type=F32)

    chunks = [slice(r, r + FFN_ROWS) for r in range(0, hn_ref.shape[0], FFN_ROWS)]
    prev = None
    for rows in chunks:
        act = up(rows)
        if prev is not None:
            down(*prev)
        prev = (rows, act)
    down(*prev)

    if final:
        @pl.when(f == pl.num_programs(1) - 1)
        def _():
            o_ref[...] = _rmsnorm_rows(o_ref[...], gf_ref[...])


def _ffn(h, g, w_gate_up, w_down, g_final, final):
    B, S, D = h.shape
    T = B * S
    tm = min(FFN_TM, T)
    h2 = h.reshape(T, D)
    nf = FF // FFN_TF
    out = pl.pallas_call(
        functools.partial(_ffn_kernel, final=final),
        out_shape=jax.ShapeDtypeStruct((T, D), F32),
        grid=(T // tm, nf),
        in_specs=[
            pl.BlockSpec((tm, D), lambda i, f: (i, 0)),
            pl.BlockSpec((1, D), lambda i, f: (0, 0)),
            pl.BlockSpec((D, FFN_TF), lambda i, f: (0, f)),
            pl.BlockSpec((D, FFN_TF), lambda i, f: (0, nf + f)),
            pl.BlockSpec((FFN_TF, D), lambda i, f: (f, 0)),
            pl.BlockSpec((1, D), lambda i, f: (0, 0)),
        ],
        out_specs=pl.BlockSpec((tm, D), lambda i, f: (i, 0)),
        scratch_shapes=[pltpu.VMEM((tm, D), BF16)],
        compiler_params=pltpu.CompilerParams(
            dimension_semantics=("parallel", "arbitrary"), vmem_limit_bytes=VMEM_LIMIT),
        name="ffn_final" if final else "ffn",
    )(h2, g, w_gate_up, w_gate_up, w_down, g_final)
    return out.reshape(B, S, D)


def _rope_tables(S):
    half = RET_DK // 2
    pos = jnp.arange(S, dtype=F32)
    inv = 1.0 / (ROPE_BASE ** (jnp.arange(half, dtype=F32) / half))
    ang = pos[:, None] * inv[None, :]
    cos = jnp.cos(ang)
    sin = jnp.sin(ang)
    return jnp.concatenate([cos, cos], axis=-1), jnp.concatenate([-sin, sin], axis=-1)


def kernel(x, norm_mix, w_in, w_ret_o, w_sb_o, w_out, norm_ffn, w_gate_up, w_down, norm_final):
    B, S, D = x.shape
    depth = w_in.shape[0]
    cos, sin = _rope_tables(S)
    log_g = jnp.log1p(-jnp.exp2(-5.0 - jnp.arange(RET_HEADS, dtype=F32)))
    log_g = jnp.broadcast_to(log_g[:, None, None], (RET_HEADS, 1, LANES))
    r = jnp.arange(SB_BLOCK)[:, None]
    c = jnp.arange(SB_BLOCK)[None, :]
    tri = (r >= c).astype(BF16)
    g_final = norm_final.reshape(1, D)

    h = x
    for layer in range(depth):
        proj = _inproj(h, norm_mix[layer].reshape(1, D), w_in[layer].astype(BF16), cos, sin)
        ret = _retention(proj, log_g)
        sb = _stick_breaking(proj, tri)
        h = _merge(h, ret, sb, proj, w_ret_o[layer].astype(BF16), w_sb_o[layer].astype(BF16),
                   w_out[layer].astype(BF16))
        h = _ffn(h, norm_ffn[layer].reshape(1, D), w_gate_up[layer].astype(BF16),
                 w_down[layer].astype(BF16), g_final, final=(layer == depth - 1))
    return h
```

```python
import functools

import numpy as np
import jax
import jax.numpy as jnp
from jax import lax
from jax.experimental import pallas as pl
from jax.experimental.pallas import tpu as pltpu

F32 = jnp.float32
BF16 = jnp.bfloat16

D_MODEL = 1024
RET_HEADS = 4
RET_DK = 128
RET_DV = 256
RET_QK = RET_HEADS * RET_DK
RET_V = RET_HEADS * RET_DV
RET_CHUNK = 128
ROPE_BASE = 10000.0
SB_HEADS = 8
SB_DH = 64
SB_W = SB_HEADS * SB_DH
SB_BLOCK = 128
FF = 2816
NORM_EPS = 1e-6
IN_COLS = 2 * RET_QK + 2 * RET_V + 3 * SB_W + 2 * D_MODEL

OFF_RQ = 0
OFF_RK = OFF_RQ + RET_QK
OFF_RV = OFF_RK + RET_QK
OFF_RG = OFF_RV + RET_V
OFF_SQ = OFF_RG + RET_V
OFF_SK = OFF_SQ + SB_W
OFF_SV = OFF_SK + SB_W
OFF_GR = OFF_SV + SB_W
OFF_GS = OFF_GR + D_MODEL

LANES = 128
PROJ_TN = 512
PROJ_ROWS = 256
FFN_TM = 2048
FFN_TF = 256
FFN_ROWS = 256
MERGE_TM = 1024
MERGE_ROWS = 256
RET_UNROLL = 16
VMEM_LIMIT = 56 * 1024 * 1024

_NT = (((1,), (1,)), ((), ()))
_TN = (((0,), (0,)), ((), ()))


def _sigmoid(x):
    return 1.0 / (1.0 + jnp.exp(-x))


def _rmsnorm_rows(x, g):
    ms = jnp.mean(x * x, axis=-1, keepdims=True)
    return x * lax.rsqrt(ms + NORM_EPS) * g


def _inproj_kernel(h_ref, g_ref, w_ref, cos_ref, sin_ref, o_ref, hn_ref):
    j = pl.program_id(1)
    S = hn_ref.shape[0]

    @pl.when(j == 0)
    def _():
        hn_ref[...] = _rmsnorm_rows(h_ref[...], g_ref[...]).astype(BF16)

    def project(epilogue):
        for r in range(0, S, PROJ_ROWS):
            rows = slice(r, r + PROJ_ROWS)
            y = jnp.dot(hn_ref[rows, :], w_ref[...], preferred_element_type=F32)
            epilogue(y, rows)

    j_rk = OFF_RK // PROJ_TN
    j_rv = OFF_RV // PROJ_TN
    j_rg = OFF_RG // PROJ_TN
    j_sq = OFF_SQ // PROJ_TN
    j_sk = OFF_SK // PROJ_TN
    j_gr = OFF_GR // PROJ_TN

    @pl.when(j < j_rv)
    def _():
        scale = jnp.where(j == j_rk, RET_DK ** -0.5, 1.0).astype(F32)

        def rope(y, rows):
            cos = cos_ref[rows, :]
            sin = sin_ref[rows, :]
            for hd in range(PROJ_TN // RET_DK):
                sl = slice(hd * RET_DK, (hd + 1) * RET_DK)
                yh = y[:, sl]
                r = yh * cos + pltpu.roll(yh, RET_DK // 2, 1) * sin
                o_ref[rows, sl] = (r * scale).astype(BF16)

        project(rope)

    @pl.when(((j >= j_rv) & (j < j_rg)) | ((j >= j_sk) & (j < j_gr)))
    def _():
        def identity(y, rows):
            o_ref[rows, :] = y.astype(BF16)

        project(identity)

    @pl.when((j >= j_rg) & (j < j_sq))
    def _():
        def silu(y, rows):
            o_ref[rows, :] = (y * _sigmoid(y)).astype(BF16)

        project(silu)

    @pl.when(j == j_sq)
    def _():
        def scaled(y, rows):
            o_ref[rows, :] = (y * (SB_DH ** -0.5)).astype(BF16)

        project(scaled)

    @pl.when(j >= j_gr)
    def _():
        def gate(y, rows):
            o_ref[rows, :] = _sigmoid(y).astype(BF16)

        project(gate)


def _inproj(h, g, w_in, cos, sin):
    B, S, D = h.shape
    return pl.pallas_call(
        _inproj_kernel,
        out_shape=jax.ShapeDtypeStruct((B, S, IN_COLS), BF16),
        grid=(B, IN_COLS // PROJ_TN),
        in_specs=[
            pl.BlockSpec((None, S, D), lambda b, j: (b, 0, 0)),
            pl.BlockSpec((1, D), lambda b, j: (0, 0)),
            pl.BlockSpec((D, PROJ_TN), lambda b, j: (0, j)),
            pl.BlockSpec((S, RET_DK), lambda b, j: (0, 0)),
            pl.BlockSpec((S, RET_DK), lambda b, j: (0, 0)),
        ],
        out_specs=pl.BlockSpec((None, S, PROJ_TN), lambda b, j: (b, 0, j)),
        scratch_shapes=[pltpu.VMEM((S, D), BF16)],
        compiler_params=pltpu.CompilerParams(
            dimension_semantics=("parallel", "arbitrary"), vmem_limit_bytes=VMEM_LIMIT),
        name="inproj",
    )(h, g, w_in, cos, sin)


def _retention_kernel(lg_ref, q_ref, k_ref, v_ref, g_ref, o_ref, kv_ref, r_ref):
    C = RET_CHUNK
    n_chunks = q_ref.shape[0] // C
    lg_row = lg_ref[...]
    lg = lg_row[:, :1]
    ii = lax.broadcasted_iota(jnp.int32, (C, C), 0)
    jj = lax.broadcasted_iota(jnp.int32, (C, C), 1)
    diff = (ii - jj).astype(F32)
    dec = jnp.where(diff >= 0, jnp.exp(lg_row * jnp.maximum(diff, 0.0)), 0.0)
    i_col = lax.broadcasted_iota(jnp.int32, (C, 1), 0).astype(F32)
    k_dec = jnp.exp(lg * (C - 1.0 - i_col))
    q_dec = jnp.exp(lg * (i_col + 1.0))
    chunk_dec = jnp.exp(lg * C)

    def chunk_rows(n):
        return pl.ds(pl.multiple_of(n * C, C), C)

    def kv_body(n, carry):
        rows = chunk_rows(n)
        kd = (k_ref[rows, :].astype(F32) * k_dec).astype(BF16)
        kv_ref[n] = lax.dot_general(kd, v_ref[rows, :], _TN, preferred_element_type=F32)
        return carry

    lax.fori_loop(0, n_chunks, kv_body, 0, unroll=RET_UNROLL)

    def scan_body(n, r):
        r_ref[n] = r.astype(BF16)
        return chunk_dec * r + kv_ref[n]

    lax.fori_loop(0, n_chunks, scan_body, jnp.zeros((RET_DK, RET_DV), F32), unroll=RET_UNROLL)

    def out_body(n, carry):
        rows = chunk_rows(n)
        q = q_ref[rows, :]
        v = v_ref[rows, :]
        s = lax.dot_general(q, k_ref[rows, :], _NT, preferred_element_type=F32) * dec
        lhs = jnp.concatenate([s.astype(BF16), (q.astype(F32) * q_dec).astype(BF16)], axis=1)
        rhs = jnp.concatenate([v, r_ref[n]], axis=0)
        o = jnp.dot(lhs, rhs, preferred_element_type=F32)
        mu = jnp.mean(o, axis=-1, keepdims=True)
        yc = o - mu
        var = jnp.mean(yc * yc, axis=-1, keepdims=True)
        y = yc * lax.rsqrt(var + NORM_EPS)
        o_ref[rows, :] = (y * g_ref[rows, :].astype(F32)).astype(BF16)
        return carry

    lax.fori_loop(0, n_chunks, out_body, 0, unroll=RET_UNROLL)


def _retention(proj, log_g):
    B, S, _ = proj.shape
    qb, kb = OFF_RQ // RET_DK, OFF_RK // RET_DK
    vb, gb = OFF_RV // RET_DV, OFF_RG // RET_DV
    return pl.pallas_call(
        _retention_kernel,
        out_shape=jax.ShapeDtypeStruct((B, S, RET_V), BF16),
        grid=(B, RET_HEADS),
        in_specs=[
            pl.BlockSpec((None, 1, LANES), lambda b, h: (h, 0, 0)),
            pl.BlockSpec((None, S, RET_DK), lambda b, h: (b, 0, qb + h)),
            pl.BlockSpec((None, S, RET_DK), lambda b, h: (b, 0, kb + h)),
            pl.BlockSpec((None, S, RET_DV), lambda b, h: (b, 0, vb + h)),
            pl.BlockSpec((None, S, RET_DV), lambda b, h: (b, 0, gb + h)),
        ],
        out_specs=pl.BlockSpec((None, S, RET_DV), lambda b, h: (b, 0, h)),
        scratch_shapes=[pltpu.VMEM((S // RET_CHUNK, RET_DK, RET_DV), F32),
                        pltpu.VMEM((S // RET_CHUNK, RET_DK, RET_DV), BF16)],
        compiler_params=pltpu.CompilerParams(
            dimension_semantics=("parallel", "parallel"), vmem_limit_bytes=VMEM_LIMIT),
        name="retention",
    )(log_g, proj, proj, proj, proj)


SB_PAIRS = SB_W // LANES
SB_STAGES = 5
SB_Z_SLOTS = 4
SB_UNROLL = 4
SB_DIAG_GROUP = 4


def _sb_schedule(n_blocks):
    tiles = [(qi, qi - s) for qi in range(1, n_blocks) for s in range(1, qi + 1)]
    pad = SB_STAGES - 1
    n_steps = -(-(len(tiles) + pad) // SB_UNROLL) * SB_UNROLL
    tab = np.zeros((3, n_steps + pad), np.int32)
    tab[1, :] = n_blocks
    for t, (qi, j) in enumerate(tiles):
        tab[:, t + pad] = (qi, qi, j)
    return tab


def _sb_kernel(tab_ref, q_ref, k_ref, v_ref, t_ref, o_ref,
               q2_ref, v2_ref, carry_ref, acc_ref, z_ref, sp_ref, cs_ref, a_ref):
    BLK = SB_BLOCK
    n_blocks = q_ref.shape[0] // BLK
    pad = SB_STAGES - 1
    n_steps = tab_ref.shape[1] - pad
    pairs = range(SB_PAIRS)
    cols = [slice(p * LANES, (p + 1) * LANES) for p in pairs]
    lane = lax.broadcasted_iota(jnp.int32, (BLK, LANES), 1)
    head0 = lane < SB_DH
    qi_loc = lax.broadcasted_iota(jnp.int32, (2 * BLK, BLK), 0) % BLK
    ki_loc = lax.broadcasted_iota(jnp.int32, (2 * BLK, BLK), 1)
    diag_mask = ki_loc < qi_loc
    sign_bit = jnp.int32(-2 ** 31)

    def block_rows(j):
        return pl.ds(pl.multiple_of(j * BLK, BLK), BLK)

    def split_heads(x):
        zero = jnp.zeros_like(x)
        return jnp.concatenate([jnp.where(head0, x, zero), jnp.where(head0, zero, x)], axis=0)

    def softplus(z):
        neg_abs = pltpu.bitcast(pltpu.bitcast(z, jnp.int32) | sign_bit, F32)
        return jnp.maximum(z, 0.0) + jnp.log(1.0 + jnp.exp(neg_abs))

    def stack_heads(a):
        ab = a.astype(BF16)
        return jnp.concatenate([ab[:BLK], ab[BLK:]], axis=1)

    def lane_bcast_col0(x):
        return jnp.broadcast_to(x[:, :1], x.shape)

    def prep(b, _):
        rows = block_rows(b)
        for p in pairs:
            q2_ref[b, p] = split_heads(q_ref[rows, cols[p]])
            v2_ref[b, p] = split_heads(v_ref[rows, cols[p]])
        return 0

    lax.fori_loop(0, n_blocks, prep, 0)
    z_ref[...] = jnp.zeros_like(z_ref)
    sp_ref[...] = jnp.zeros_like(sp_ref)
    cs_ref[...] = jnp.zeros_like(cs_ref)
    a_ref[...] = jnp.zeros_like(a_ref)
    carry_ref[n_blocks] = jnp.zeros(carry_ref.shape[1:], F32)
    acc_ref[n_blocks] = jnp.zeros(acc_ref.shape[1:], F32)

    def diag(i, _):
        work = [(SB_DIAG_GROUP * i + d, p) for d in range(SB_DIAG_GROUP) for p in pairs]
        zs = [lax.dot_general(q2_ref[qi, p], k_ref[block_rows(qi), cols[p]], _NT,
                              preferred_element_type=F32) for qi, p in work]
        sps = [jnp.where(diag_mask, softplus(z), 0.0).astype(BF16) for z in zs]
        css = [jnp.dot(sp, t_ref[...], preferred_element_type=F32) for sp in sps]
        a2s = []
        for (qi, p), z, suffix in zip(work, zs, css):
            a2s.append(stack_heads(jnp.where(diag_mask, jnp.exp(z - suffix), 0.0)))
            carry_ref[qi, p] = lane_bcast_col0(suffix)
        for (qi, p), a2 in zip(work, a2s):
            acc_ref[qi, p] = jnp.dot(a2, v2_ref[qi, p], preferred_element_type=F32)
        return 0

    lax.fori_loop(0, n_blocks // SB_DIAG_GROUP, diag, 0)

    def step(s, u):
        qi_qk, j_qk = tab_ref[0, s + pad], tab_ref[2, s + pad]
        slot_exp = tab_ref[1, s + pad - 3]
        slot_av, j_av = tab_ref[1, s], tab_ref[2, s]
        zslot_qk = u % SB_Z_SLOTS
        zslot_sp = (u - 1) % SB_Z_SLOTS
        zslot_exp = (u - 3) % SB_Z_SLOTS
        rows_qk = block_rows(j_qk)
        pvs = [jnp.dot(a_ref[p], v2_ref[j_av, p], preferred_element_type=F32) for p in pairs]
        css = [jnp.dot(sp_ref[p], t_ref[...], preferred_element_type=F32) for p in pairs]
        zs = [lax.dot_general(q2_ref[qi_qk, p], k_ref[rows_qk, cols[p]], _NT,
                              preferred_element_type=F32) for p in pairs]
        sps = [softplus(z_ref[zslot_sp, p]).astype(BF16) for p in pairs]
        a2s, carries = [], []
        for p in pairs:
            suffix = cs_ref[p] + carry_ref[slot_exp, p]
            a2s.append(stack_heads(jnp.exp(z_ref[zslot_exp, p] - suffix)))
            carries.append(lane_bcast_col0(suffix))
        for p in pairs:
            sp_ref[p] = sps[p]
            a_ref[p] = a2s[p]
            carry_ref[slot_exp, p] = carries[p]
            cs_ref[p] = css[p]
            z_ref[zslot_qk, p] = zs[p]
            acc_ref[slot_av, p] += pvs[p]

    def steps(i, _):
        for u in range(SB_UNROLL):
            step(i * SB_UNROLL + u, u)
        return 0

    lax.fori_loop(0, n_steps // SB_UNROLL, steps, 0)

    def emit(b, _):
        rows = block_rows(b)
        for p in pairs:
            o_ref[rows, cols[p]] = acc_ref[b, p].astype(BF16)
        return 0

    lax.fori_loop(0, n_blocks, emit, 0)


def _stick_breaking(proj, tri):
    B, S, _ = proj.shape
    n_blocks = S // SB_BLOCK
    assert n_blocks % SB_DIAG_GROUP == 0
    qb, kb, vb = OFF_SQ // SB_W, OFF_SK // SB_W, OFF_SV // SB_W
    tab = jnp.asarray(_sb_schedule(n_blocks))
    stacked = (SB_PAIRS, 2 * SB_BLOCK, SB_BLOCK)
    grid_spec = pltpu.PrefetchScalarGridSpec(
        num_scalar_prefetch=1,
        grid=(B,),
        in_specs=[
            pl.BlockSpec((None, S, SB_W), lambda b, tab: (b, 0, qb)),
            pl.BlockSpec((None, S, SB_W), lambda b, tab: (b, 0, kb)),
            pl.BlockSpec((None, S, SB_W), lambda b, tab: (b, 0, vb)),
            pl.BlockSpec((SB_BLOCK, SB_BLOCK), lambda b, tab: (0, 0)),
        ],
        out_specs=pl.BlockSpec((None, S, SB_W), lambda b, tab: (b, 0, 0)),
        scratch_shapes=[
            pltpu.VMEM((n_blocks,) + stacked, BF16),
            pltpu.VMEM((n_blocks,) + stacked, BF16),
            pltpu.VMEM((n_blocks + 1,) + stacked, F32),
            pltpu.VMEM((n_blocks + 1, SB_PAIRS, SB_BLOCK, LANES), F32),
            pltpu.VMEM((SB_Z_SLOTS,) + stacked, F32),
            pltpu.VMEM(stacked, BF16),
            pltpu.VMEM(stacked, F32),
            pltpu.VMEM((SB_PAIRS, SB_BLOCK, 2 * SB_BLOCK), BF16),
        ],
    )
    return pl.pallas_call(
        _sb_kernel,
        out_shape=jax.ShapeDtypeStruct((B, S, SB_W), BF16),
        grid_spec=grid_spec,
        compiler_params=pltpu.CompilerParams(
            dimension_semantics=("parallel",), vmem_limit_bytes=VMEM_LIMIT),
        name="stick_breaking",
    )(tab, proj, proj, proj, tri)


def _merge_kernel(h_ref, ret_ref, sb_ref, gr0_ref, gr1_ref, gs0_ref, gs1_ref,
                  wr_ref, ws_ref, wo_ref, o_ref):
    def branches(rows):
        ro = jnp.dot(ret_ref[rows, :], wr_ref[...], preferred_element_type=F32)
        so = jnp.dot(sb_ref[rows, :], ws_ref[...], preferred_element_type=F32)
        gr = jnp.concatenate([gr0_ref[rows, :], gr1_ref[rows, :]], axis=1).astype(F32)
        gs = jnp.concatenate([gs0_ref[rows, :], gs1_ref[rows, :]], axis=1).astype(F32)
        return (gr * ro + gs * so).astype(BF16)

    def project(rows, merged):
        o_ref[rows, :] = h_ref[rows, :] + jnp.dot(merged, wo_ref[...], preferred_element_type=F32)

    chunks = [slice(r, r + MERGE_ROWS) for r in range(0, o_ref.shape[0], MERGE_ROWS)]
    prev = None
    for rows in chunks:
        merged = branches(rows)
        if prev is not None:
            project(*prev)
        prev = (rows, merged)
    project(*prev)


def _merge(h, ret, sb, proj, w_ret_o, w_sb_o, w_out):
    B, S, D = h.shape
    tm = min(MERGE_TM, S)
    half = D_MODEL // 2
    gr0, gs0 = OFF_GR // half, OFF_GS // half
    tok = lambda width, col: pl.BlockSpec((None, tm, width), lambda b, i: (b, i, col))
    full = lambda a: pl.BlockSpec(a.shape, lambda b, i: (0, 0))
    return pl.pallas_call(
        _merge_kernel,
        out_shape=jax.ShapeDtypeStruct((B, S, D), F32),
        grid=(B, S // tm),
        in_specs=[tok(D, 0), tok(RET_V, 0), tok(SB_W, 0),
                  tok(half, gr0), tok(half, gr0 + 1), tok(half, gs0), tok(half, gs0 + 1),
                  full(w_ret_o), full(w_sb_o), full(w_out)],
        out_specs=tok(D, 0),
        compiler_params=pltpu.CompilerParams(
            dimension_semantics=("parallel", "parallel"), vmem_limit_bytes=VMEM_LIMIT),
        name="merge",
    )(h, ret, sb, proj, proj, proj, proj, w_ret_o, w_sb_o, w_out)


def _ffn_kernel(h_ref, g_ref, wg_ref, wu_ref, wd_ref, gf_ref, o_ref, hn_ref, *, final):
    f = pl.program_id(1)

    @pl.when(f == 0)
    def _():
        x = h_ref[...]
        hn_ref[...] = _rmsnorm_rows(x, g_ref[...]).astype(BF16)
        o_ref[...] = x

    def up(rows):
        hn = hn_ref[rows, :]
        g = jnp.dot(hn, wg_ref[...], preferred_element_type=F32)
        u = jnp.dot(hn, wu_ref[...], preferred_element_type=F32)
        return (g * _sigmoid(g) * u).astype(BF16)

    def down(rows, act):
        o_ref[rows, :] += jnp.dot(act, wd_ref[...], preferred_element_type=F32)

    chunks = [slice(r, r + FFN_ROWS) for r in range(0, hn_ref.shape[0], FFN_ROWS)]
    prev = None
    for rows in chunks:
        act = up(rows)
        if prev is not None:
            down(*prev)
        prev = (rows, act)
    down(*prev)

    if final:
        @pl.when(f == pl.num_programs(1) - 1)
        def _():
            o_ref[...] = _rmsnorm_rows(o_ref[...], gf_ref[...])


def _ffn(h, g, w_gate_up, w_down, g_final, final):
    B, S, D = h.shape
    T = B * S
    tm = min(FFN_TM, T)
    h2 = h.reshape(T, D)
    nf = FF // FFN_TF
    out = pl.pallas_call(
        functools.partial(_ffn_kernel, final=final),
        out_shape=jax.ShapeDtypeStruct((T, D), F32),
        grid=(T // tm, nf),
        in_specs=[
            pl.BlockSpec((tm, D), lambda i, f: (i, 0)),
            pl.BlockSpec((1, D), lambda i, f: (0, 0)),
            pl.BlockSpec((D, FFN_TF), lambda i, f: (0, f)),
            pl.BlockSpec((D, FFN_TF), lambda i, f: (0, nf + f)),
            pl.BlockSpec((FFN_TF, D), lambda i, f: (f, 0)),
            pl.BlockSpec((1, D), lambda i, f: (0, 0)),
        ],
        out_specs=pl.BlockSpec((tm, D), lambda i, f: (i, 0)),
        scratch_shapes=[pltpu.VMEM((tm, D), BF16)],
        compiler_params=pltpu.CompilerParams(
            dimension_semantics=("parallel", "arbitrary"), vmem_limit_bytes=VMEM_LIMIT),
        name="ffn_final" if final else "ffn",
    )(h2, g, w_gate_up, w_gate_up, w_down, g_final)
    return out.reshape(B, S, D)


def _rope_tables(S):
    half = RET_DK // 2
    pos = jnp.arange(S, dtype=F32)
    inv = 1.0 / (ROPE_BASE ** (jnp.arange(half, dtype=F32) / half))
    ang = pos[:, None] * inv[None, :]
    cos = jnp.cos(ang)
    sin = jnp.sin(ang)
    return jnp.concatenate([cos, cos], axis=-1), jnp.concatenate([-sin, sin], axis=-1)


def kernel(x, norm_mix, w_in, w_ret_o, w_sb_o, w_out, norm_ffn, w_gate_up, w_down, norm_final):
    B, S, D = x.shape
    depth = w_in.shape[0]
    cos, sin = _rope_tables(S)
    log_g = jnp.log1p(-jnp.exp2(-5.0 - jnp.arange(RET_HEADS, dtype=F32)))
    log_g = jnp.broadcast_to(log_g[:, None, None], (RET_HEADS, 1, LANES))
    r = jnp.arange(SB_BLOCK)[:, None]
    c = jnp.arange(SB_BLOCK)[None, :]
    tri = (r >= c).astype(BF16)
    g_final = norm_final.reshape(1, D)

    h = x
    for layer in range(depth):
        proj = _inproj(h, norm_mix[layer].reshape(1, D), w_in[layer].astype(BF16), cos, sin)
        ret = _retention(proj, log_g)
        sb = _stick_breaking(proj, tri)
        h = _merge(h, ret, sb, proj, w_ret_o[layer].astype(BF16), w_sb_o[layer].astype(BF16),
                   w_out[layer].astype(BF16))
        h = _ffn(h, norm_ffn[layer].reshape(1, D), w_gate_up[layer].astype(BF16),
                 w_down[layer].astype(BF16), g_final, final=(layer == depth - 1))
    return h
```

```python
import functools

import numpy as np
import jax
import jax.numpy as jnp
from jax import lax
from jax.experimental import pallas as pl
from jax.experimental.pallas import tpu as pltpu

F32 = jnp.float32
BF16 = jnp.bfloat16

D_MODEL = 1024
RET_HEADS = 4
RET_DK = 128
RET_DV = 256
RET_QK = RET_HEADS * RET_DK
RET_V = RET_HEADS * RET_DV
RET_CHUNK = 128
ROPE_BASE = 10000.0
SB_HEADS = 8
SB_DH = 64
SB_W = SB_HEADS * SB_DH
SB_BLOCK = 128
FF = 2816
NORM_EPS = 1e-6
IN_COLS = 2 * RET_QK + 2 * RET_V + 3 * SB_W + 2 * D_MODEL

OFF_RQ = 0
OFF_RK = OFF_RQ + RET_QK
OFF_RV = OFF_RK + RET_QK
OFF_RG = OFF_RV + RET_V
OFF_SQ = OFF_RG + RET_V
OFF_SK = OFF_SQ + SB_W
OFF_SV = OFF_SK + SB_W
OFF_GR = OFF_SV + SB_W
OFF_GS = OFF_GR + D_MODEL

LANES = 128
PROJ_TN = 512
PROJ_ROWS = 256
FFN_TM = 2048
FFN_TF = 256
FFN_ROWS = 256
MERGE_TM = 1024
MERGE_ROWS = 256
RET_UNROLL = 16
VMEM_LIMIT = 56 * 1024 * 1024

_NT = (((1,), (1,)), ((), ()))
_TN = (((0,), (0,)), ((), ()))


def _sigmoid(x):
    return 1.0 / (1.0 + jnp.exp(-x))


def _rmsnorm_rows(x, g):
    ms = jnp.mean(x * x, axis=-1, keepdims=True)
    return x * lax.rsqrt(ms + NORM_EPS) * g


def _inproj_kernel(h_ref, g_ref, w_ref, cos_ref, sin_ref, o_ref, hn_ref):
    j = pl.program_id(1)
    S = hn_ref.shape[0]

    @pl.when(j == 0)
    def _():
        hn_ref[...] = _rmsnorm_rows(h_ref[...], g_ref[...]).astype(BF16)

    def project(epilogue):
        w = w_ref[...].astype(BF16)
        for r in range(0, S, PROJ_ROWS):
            rows = slice(r, r + PROJ_ROWS)
            y = jnp.dot(hn_ref[rows, :], w, preferred_element_type=F32)
            epilogue(y, rows)

    j_rk = OFF_RK // PROJ_TN
    j_rv = OFF_RV // PROJ_TN
    j_rg = OFF_RG // PROJ_TN
    j_sq = OFF_SQ // PROJ_TN
    j_sk = OFF_SK // PROJ_TN
    j_gr = OFF_GR // PROJ_TN

    @pl.when(j < j_rv)
    def _():
        scale = jnp.where(j == j_rk, RET_DK ** -0.5, 1.0).astype(F32)

        def rope(y, rows):
            cos = cos_ref[rows, :]
            sin = sin_ref[rows, :]
            for hd in range(PROJ_TN // RET_DK):
                sl = slice(hd * RET_DK, (hd + 1) * RET_DK)
                yh = y[:, sl]
                r = yh * cos + pltpu.roll(yh, RET_DK // 2, 1) * sin
                o_ref[rows, sl] = (r * scale).astype(BF16)

        project(rope)

    @pl.when(((j >= j_rv) & (j < j_rg)) | ((j >= j_sk) & (j < j_gr)))
    def _():
        def identity(y, rows):
            o_ref[rows, :] = y.astype(BF16)

        project(identity)

    @pl.when((j >= j_rg) & (j < j_sq))
    def _():
        def silu(y, rows):
            o_ref[rows, :] = (y * _sigmoid(y)).astype(BF16)

        project(silu)

    @pl.when(j == j_sq)
    def _():
        def scaled(y, rows):
            o_ref[rows, :] = (y * (SB_DH ** -0.5)).astype(BF16)

        project(scaled)

    @pl.when(j >= j_gr)
    def _():
        def gate(y, rows):
            o_ref[rows, :] = _sigmoid(y).astype(BF16)

        project(gate)


def _inproj(h, g, w_in, layer, cos, sin):
    B, S, D = h.shape
    return pl.pallas_call(
        _inproj_kernel,
        out_shape=jax.ShapeDtypeStruct((B, S, IN_COLS), BF16),
        grid=(B, IN_COLS // PROJ_TN),
        in_specs=[
            pl.BlockSpec((None, S, D), lambda b, j: (b, 0, 0)),
            pl.BlockSpec((1, D), lambda b, j: (0, 0)),
            pl.BlockSpec((None, D, PROJ_TN), lambda b, j: (layer, 0, j)),
            pl.BlockSpec((S, RET_DK), lambda b, j: (0, 0)),
            pl.BlockSpec((S, RET_DK), lambda b, j: (0, 0)),
        ],
        out_specs=pl.BlockSpec((None, S, PROJ_TN), lambda b, j: (b, 0, j)),
        scratch_shapes=[pltpu.VMEM((S, D), BF16)],
        compiler_params=pltpu.CompilerParams(
            dimension_semantics=("parallel", "arbitrary"), vmem_limit_bytes=VMEM_LIMIT),
        name="inproj",
    )(h, g, w_in, cos, sin)


def _retention_kernel(lg_ref, q_ref, k_ref, v_ref, g_ref, o_ref, kv_ref, r_ref):
    C = RET_CHUNK
    n_chunks = q_ref.shape[0] // C
    lg_row = lg_ref[...]
    lg = lg_row[:, :1]
    ii = lax.broadcasted_iota(jnp.int32, (C, C), 0)
    jj = lax.broadcasted_iota(jnp.int32, (C, C), 1)
    diff = (ii - jj).astype(F32)
    dec = jnp.where(diff >= 0, jnp.exp(lg_row * jnp.maximum(diff, 0.0)), 0.0)
    i_col = lax.broadcasted_iota(jnp.int32, (C, 1), 0).astype(F32)
    k_dec = jnp.exp(lg * (C - 1.0 - i_col))
    q_dec = jnp.exp(lg * (i_col + 1.0))
    chunk_dec = jnp.exp(lg * C)

    def chunk_rows(n):
        return pl.ds(pl.multiple_of(n * C, C), C)

    def kv_body(n, carry):
        rows = chunk_rows(n)
        kd = (k_ref[rows, :].astype(F32) * k_dec).astype(BF16)
        kv_ref[n] = lax.dot_general(kd, v_ref[rows, :], _TN, preferred_element_type=F32)
        return carry

    lax.fori_loop(0, n_chunks, kv_body, 0, unroll=RET_UNROLL)

    def scan_body(n, r):
        r_ref[n] = r.astype(BF16)
        return chunk_dec * r + kv_ref[n]

    lax.fori_loop(0, n_chunks, scan_body, jnp.zeros((RET_DK, RET_DV), F32), unroll=RET_UNROLL)

    def out_body(n, carry):
        rows = chunk_rows(n)
        q = q_ref[rows, :]
        v = v_ref[rows, :]
        s = lax.dot_general(q, k_ref[rows, :], _NT, preferred_element_type=F32) * dec
        lhs = jnp.concatenate([s.astype(BF16), (q.astype(F32) * q_dec).astype(BF16)], axis=1)
        rhs = jnp.concatenate([v, r_ref[n]], axis=0)
        o = jnp.dot(lhs, rhs, preferred_element_type=F32)
        mu = jnp.mean(o, axis=-1, keepdims=True)
        yc = o - mu
        var = jnp.mean(yc * yc, axis=-1, keepdims=True)
        y = yc * lax.rsqrt(var + NORM_EPS)
        o_ref[rows, :] = (y * g_ref[rows, :].astype(F32)).astype(BF16)
        return carry

    lax.fori_loop(0, n_chunks, out_body, 0, unroll=RET_UNROLL)


def _retention(proj, log_g):
    B, S, _ = proj.shape
    qb, kb = OFF_RQ // RET_DK, OFF_RK // RET_DK
    vb, gb = OFF_RV // RET_DV, OFF_RG // RET_DV
    return pl.pallas_call(
        _retention_kernel,
        out_shape=jax.ShapeDtypeStruct((B, S, RET_V), BF16),
        grid=(B, RET_HEADS),
        in_specs=[
            pl.BlockSpec((None, 1, LANES), lambda b, h: (h, 0, 0)),
            pl.BlockSpec((None, S, RET_DK), lambda b, h: (b, 0, qb + h)),
            pl.BlockSpec((None, S, RET_DK), lambda b, h: (b, 0, kb + h)),
            pl.BlockSpec((None, S, RET_DV), lambda b, h: (b, 0, vb + h)),
            pl.BlockSpec((None, S, RET_DV), lambda b, h: (b, 0, gb + h)),
        ],
        out_specs=pl.BlockSpec((None, S, RET_DV), lambda b, h: (b, 0, h)),
        scratch_shapes=[pltpu.VMEM((S // RET_CHUNK, RET_DK, RET_DV), F32),
                        pltpu.VMEM((S // RET_CHUNK, RET_DK, RET_DV), BF16)],
        compiler_params=pltpu.CompilerParams(
            dimension_semantics=("parallel", "parallel"), vmem_limit_bytes=VMEM_LIMIT),
        name="retention",
    )(log_g, proj, proj, proj, proj)


SB_PAIRS = SB_W // LANES
SB_STAGES = 5
SB_Z_SLOTS = 4
SB_UNROLL = 4
SB_DIAG_GROUP = 4


def _sb_schedule(n_blocks):
    tiles = [(qi, qi - s) for qi in range(1, n_blocks) for s in range(1, qi + 1)]
    pad = SB_STAGES - 1
    n_steps = -(-(len(tiles) + pad) // SB_UNROLL) * SB_UNROLL
    tab = np.zeros((3, n_steps + pad), np.int32)
    tab[1, :] = n_blocks
    for t, (qi, j) in enumerate(tiles):
        tab[:, t + pad] = (qi, qi, j)
    return tab


def _sb_kernel(tab_ref, q_ref, k_ref, v_ref, t_ref, o_ref,
               q2_ref, v2_ref, carry_ref, acc_ref, z_ref, sp_ref, cs_ref, a_ref):
    BLK = SB_BLOCK
    n_blocks = q_ref.shape[0] // BLK
    pad = SB_STAGES - 1
    n_steps = tab_ref.shape[1] - pad
    pairs = range(SB_PAIRS)
    cols = [slice(p * LANES, (p + 1) * LANES) for p in pairs]
    lane = lax.broadcasted_iota(jnp.int32, (BLK, LANES), 1)
    head0 = lane < SB_DH
    qi_loc = lax.broadcasted_iota(jnp.int32, (2 * BLK, BLK), 0) % BLK
    ki_loc = lax.broadcasted_iota(jnp.int32, (2 * BLK, BLK), 1)
    diag_mask = ki_loc < qi_loc
    sign_bit = jnp.int32(-2 ** 31)

    def block_rows(j):
        return pl.ds(pl.multiple_of(j * BLK, BLK), BLK)

    def split_heads(x):
        zero = jnp.zeros_like(x)
        return jnp.concatenate([jnp.where(head0, x, zero), jnp.where(head0, zero, x)], axis=0)

    def softplus(z):
        neg_abs = pltpu.bitcast(pltpu.bitcast(z, jnp.int32) | sign_bit, F32)
        return jnp.maximum(z, 0.0) + jnp.log(1.0 + jnp.exp(neg_abs))

    def softplus_bf16(z):
        zb = z.astype(BF16)
        return jnp.maximum(zb, 0.0) + jnp.log(1.0 + jnp.exp(-jnp.abs(zb)))

    def stack_heads(a):
        ab = a.astype(BF16)
        return jnp.concatenate([ab[:BLK], ab[BLK:]], axis=1)

    def lane_bcast_col0(x):
        return jnp.broadcast_to(x[:, :1], x.shape)

    def prep(b, _):
        rows = block_rows(b)
        for p in pairs:
            q2_ref[b, p] = split_heads(q_ref[rows, cols[p]])
            v2_ref[b, p] = split_heads(v_ref[rows, cols[p]])
        return 0

    lax.fori_loop(0, n_blocks, prep, 0)
    z_ref[...] = jnp.zeros_like(z_ref)
    sp_ref[...] = jnp.zeros_like(sp_ref)
    cs_ref[...] = jnp.zeros_like(cs_ref)
    a_ref[...] = jnp.zeros_like(a_ref)
    carry_ref[n_blocks] = jnp.zeros(carry_ref.shape[1:], F32)
    acc_ref[n_blocks] = jnp.zeros(acc_ref.shape[1:], F32)

    def diag(i, _):
        work = [(SB_DIAG_GROUP * i + d, p) for d in range(SB_DIAG_GROUP) for p in pairs]
        zs = [lax.dot_general(q2_ref[qi, p], k_ref[block_rows(qi), cols[p]], _NT,
                              preferred_element_type=F32) for qi, p in work]
        sps = [jnp.where(diag_mask, softplus(z), 0.0).astype(BF16) for z in zs]
        css = [jnp.dot(sp, t_ref[...], preferred_element_type=F32) for sp in sps]
        a2s = []
        for (qi, p), z, suffix in zip(work, zs, css):
            a2s.append(stack_heads(jnp.where(diag_mask, jnp.exp(z - suffix), 0.0)))
            carry_ref[qi, p] = lane_bcast_col0(suffix)
        for (qi, p), a2 in zip(work, a2s):
            acc_ref[qi, p] = jnp.dot(a2, v2_ref[qi, p], preferred_element_type=F32)
        return 0

    lax.fori_loop(0, n_blocks // SB_DIAG_GROUP, diag, 0)

    def step(s, u):
        qi_qk, j_qk = tab_ref[0, s + pad], tab_ref[2, s + pad]
        slot_exp = tab_ref[1, s + pad - 3]
        slot_av, j_av = tab_ref[1, s], tab_ref[2, s]
        zslot_qk = u % SB_Z_SLOTS
        zslot_sp = (u - 1) % SB_Z_SLOTS
        zslot_exp = (u - 3) % SB_Z_SLOTS
        rows_qk = block_rows(j_qk)
        pvs = [jnp.dot(a_ref[p], v2_ref[j_av, p], preferred_element_type=F32) for p in pairs]
        css = [jnp.dot(sp_ref[p], t_ref[...], preferred_element_type=F32) for p in pairs]
        zs = [lax.dot_general(q2_ref[qi_qk, p], k_ref[rows_qk, cols[p]], _NT,
                              preferred_element_type=F32) for p in pairs]
        sps = [softplus_bf16(z_ref[zslot_sp, p]) for p in pairs]
        a2s, carries = [], []
        for p in pairs:
            suffix = cs_ref[p] + carry_ref[slot_exp, p]
            a2s.append(stack_heads(jnp.exp(z_ref[zslot_exp, p] - suffix)))
            carries.append(lane_bcast_col0(suffix))
        for p in pairs:
            sp_ref[p] = sps[p]
            a_ref[p] = a2s[p]
            carry_ref[slot_exp, p] = carries[p]
            cs_ref[p] = css[p]
            z_ref[zslot_qk, p] = zs[p]
            acc_ref[slot_av, p] += pvs[p]

    def steps(i, _):
        for u in range(SB_UNROLL):
            step(i * SB_UNROLL + u, u)
        return 0

    lax.fori_loop(0, n_steps // SB_UNROLL, steps, 0)

    def emit(b, _):
        rows = block_rows(b)
        for p in pairs:
            o_ref[rows, cols[p]] = acc_ref[b, p].astype(BF16)
        return 0

    lax.fori_loop(0, n_blocks, emit, 0)


def _stick_breaking(proj, tri):
    B, S, _ = proj.shape
    n_blocks = S // SB_BLOCK
    assert n_blocks % SB_DIAG_GROUP == 0
    qb, kb, vb = OFF_SQ // SB_W, OFF_SK // SB_W, OFF_SV // SB_W
    tab = jnp.asarray(_sb_schedule(n_blocks))
    stacked = (SB_PAIRS, 2 * SB_BLOCK, SB_BLOCK)
    grid_spec = pltpu.PrefetchScalarGridSpec(
        num_scalar_prefetch=1,
        grid=(B,),
        in_specs=[
            pl.BlockSpec((None, S, SB_W), lambda b, tab: (b, 0, qb)),
            pl.BlockSpec((None, S, SB_W), lambda b, tab: (b, 0, kb)),
            pl.BlockSpec((None, S, SB_W), lambda b, tab: (b, 0, vb)),
            pl.BlockSpec((SB_BLOCK, SB_BLOCK), lambda b, tab: (0, 0)),
        ],
        out_specs=pl.BlockSpec((None, S, SB_W), lambda b, tab: (b, 0, 0)),
        scratch_shapes=[
            pltpu.VMEM((n_blocks,) + stacked, BF16),
            pltpu.VMEM((n_blocks,) + stacked, BF16),
            pltpu.VMEM((n_blocks + 1,) + stacked, F32),
            pltpu.VMEM((n_blocks + 1, SB_PAIRS, SB_BLOCK, LANES), F32),
            pltpu.VMEM((SB_Z_SLOTS,) + stacked, F32),
            pltpu.VMEM(stacked, BF16),
            pltpu.VMEM(stacked, F32),
            pltpu.VMEM((SB_PAIRS, SB_BLOCK, 2 * SB_BLOCK), BF16),
        ],
    )
    return pl.pallas_call(
        _sb_kernel,
        out_shape=jax.ShapeDtypeStruct((B, S, SB_W), BF16),
        grid_spec=grid_spec,
        compiler_params=pltpu.CompilerParams(
            dimension_semantics=("parallel",), vmem_limit_bytes=VMEM_LIMIT),
        name="stick_breaking",
    )(tab, proj, proj, proj, tri)


def _merge_kernel(h_ref, ret_ref, sb_ref, gr0_ref, gr1_ref, gs0_ref, gs1_ref,
                  wr_ref, ws_ref, wo_ref, o_ref):
    wr = wr_ref[...].astype(BF16)
    ws = ws_ref[...].astype(BF16)
    wo = wo_ref[...].astype(BF16)

    def branches(rows):
        ro = jnp.dot(ret_ref[rows, :], wr, preferred_element_type=F32)
        so = jnp.dot(sb_ref[rows, :], ws, preferred_element_type=F32)
        gr = jnp.concatenate([gr0_ref[rows, :], gr1_ref[rows, :]], axis=1).astype(F32)
        gs = jnp.concatenate([gs0_ref[rows, :], gs1_ref[rows, :]], axis=1).astype(F32)
        return (gr * ro + gs * so).astype(BF16)

    def project(rows, merged):
        o_ref[rows, :] = h_ref[rows, :] + jnp.dot(merged, wo, preferred_element_type=F32)

    chunks = [slice(r, r + MERGE_ROWS) for r in range(0, o_ref.shape[0], MERGE_ROWS)]
    prev = None
    for rows in chunks:
        merged = branches(rows)
        if prev is not None:
            project(*prev)
        prev = (rows, merged)
    project(*prev)


def _merge(h, ret, sb, proj, w_ret_o, w_sb_o, w_out, layer):
    B, S, D = h.shape
    tm = min(MERGE_TM, S)
    half = D_MODEL // 2
    gr0, gs0 = OFF_GR // half, OFF_GS // half
    tok = lambda width, col: pl.BlockSpec((None, tm, width), lambda b, i: (b, i, col))
    full = lambda a: pl.BlockSpec((None,) + a.shape[1:], lambda b, i: (layer, 0, 0),
                                  pipeline_mode=pl.Buffered(1))
    return pl.pallas_call(
        _merge_kernel,
        out_shape=jax.ShapeDtypeStruct((B, S, D), F32),
        grid=(B, S // tm),
        in_specs=[tok(D, 0), tok(RET_V, 0), tok(SB_W, 0),
                  tok(half, gr0), tok(half, gr0 + 1), tok(half, gs0), tok(half, gs0 + 1),
                  full(w_ret_o), full(w_sb_o), full(w_out)],
        out_specs=tok(D, 0),
        compiler_params=pltpu.CompilerParams(
            dimension_semantics=("parallel", "parallel"), vmem_limit_bytes=VMEM_LIMIT),
        name="merge",
    )(h, ret, sb, proj, proj, proj, proj, w_ret_o, w_sb_o, w_out)


def _ffn_kernel(h_ref, g_ref, wg_ref, wu_ref, wd_ref, gf_ref, o_ref, hn_ref, *, final):
    f = pl.program_id(1)

    @pl.when(f == 0)
    def _():
        x = h_ref[...]
        hn_ref[...] = _rmsnorm_rows(x, g_ref[...]).astype(BF16)
        o_ref[...] = x

    wg = wg_ref[...].astype(BF16)
    wu = wu_ref[...].astype(BF16)
    wd = wd_ref[...].astype(BF16)

    def up(rows):
        hn = hn_ref[rows, :]
        g = jnp.dot(hn, wg, preferred_element_type=F32)
        u = jnp.dot(hn, wu, preferred_element_type=F32)
        return (g * _sigmoid(g) * u).astype(BF16)

    def down(rows, act):
        o_ref[rows, :] += jnp.dot(act, wd, preferred_element_type=F32)

    chunks = [slice(r, r + FFN_ROWS) for r in range(0, hn_ref.shape[0], FFN_ROWS)]
    prev = None
    for rows in chunks:
        act = up(rows)
        if prev is not None:
            down(*prev)
        prev = (rows, act)
    down(*prev)

    if final:
        @pl.when(f == pl.num_programs(1) - 1)
        def _():
            o_ref[...] = _rmsnorm_rows(o_ref[...], gf_ref[...])


def _ffn(h, g, w_gate_up, w_down, layer, g_final, final):
    B, S, D = h.shape
    T = B * S
    tm = min(FFN_TM, T)
    h2 = h.reshape(T, D)
    nf = FF // FFN_TF
    out = pl.pallas_call(
        functools.partial(_ffn_kernel, final=final),
        out_shape=jax.ShapeDtypeStruct((T, D), F32),
        grid=(T // tm, nf),
        in_specs=[
            pl.BlockSpec((tm, D), lambda i, f: (i, 0)),
            pl.BlockSpec((1, D), lambda i, f: (0, 0)),
            pl.BlockSpec((None, D, FFN_TF), lambda i, f: (layer, 0, f)),
            pl.BlockSpec((None, D, FFN_TF), lambda i, f: (layer, 0, nf + f)),
            pl.BlockSpec((None, FFN_TF, D), lambda i, f: (layer, f, 0)),
            pl.BlockSpec((1, D), lambda i, f: (0, 0)),
        ],
        out_specs=pl.BlockSpec((tm, D), lambda i, f: (i, 0)),
        scratch_shapes=[pltpu.VMEM((tm, D), BF16)],
        compiler_params=pltpu.CompilerParams(
            dimension_semantics=("parallel", "arbitrary"), vmem_limit_bytes=VMEM_LIMIT),
        name="ffn_final" if final else "ffn",
    )(h2, g, w_gate_up, w_gate_up, w_down, g_final)
    return out.reshape(B, S, D)


def _rope_tables(S):
    half = RET_DK // 2
    pos = jnp.arange(S, dtype=F32)
    inv = 1.0 / (ROPE_BASE ** (jnp.arange(half, dtype=F32) / half))
    ang = pos[:, None] * inv[None, :]
    cos = jnp.cos(ang)
    sin = jnp.sin(ang)
    return jnp.concatenate([cos, cos], axis=-1), jnp.concatenate([-sin, sin], axis=-1)


def kernel(x, norm_mix, w_in, w_ret_o, w_sb_o, w_out, norm_ffn, w_gate_up, w_down, norm_final):
    B, S, D = x.shape
    depth = w_in.shape[0]
    cos, sin = _rope_tables(S)
    log_g = jnp.log1p(-jnp.exp2(-5.0 - jnp.arange(RET_HEADS, dtype=F32)))
    log_g = jnp.broadcast_to(log_g[:, None, None], (RET_HEADS, 1, LANES))
    r = jnp.arange(SB_BLOCK)[:, None]
    c = jnp.arange(SB_BLOCK)[None, :]
    tri = (r >= c).astype(BF16)
    g_final = norm_final.reshape(1, D)

    h = x
    for layer in range(depth):
        proj = _inproj(h, norm_mix[layer].reshape(1, D), w_in, layer, cos, sin)
        ret = _retention(proj, log_g)
        sb = _stick_breaking(proj, tri)
        h = _merge(h, ret, sb, proj, w_ret_o, w_sb_o, w_out, layer)
        h = _ffn(h, norm_ffn[layer].reshape(1, D), w_gate_up, w_down, layer, g_final,
                 final=(layer == depth - 1))
    return h
```

```python
import functools

import numpy as np
import jax
import jax.numpy as jnp
from jax import lax
from jax.experimental import pallas as pl
from jax.experimental.pallas import tpu as pltpu

F32 = jnp.float32
BF16 = jnp.bfloat16

D_MODEL = 1024
RET_HEADS = 4
RET_DK = 128
RET_DV = 256
RET_QK = RET_HEADS * RET_DK
RET_V = RET_HEADS * RET_DV
RET_CHUNK = 128
ROPE_BASE = 10000.0
SB_HEADS = 8
SB_DH = 64
SB_W = SB_HEADS * SB_DH
SB_BLOCK = 128
FF = 2816
NORM_EPS = 1e-6
IN_COLS = 2 * RET_QK + 2 * RET_V + 3 * SB_W + 2 * D_MODEL

OFF_RQ = 0
OFF_RK = OFF_RQ + RET_QK
OFF_RV = OFF_RK + RET_QK
OFF_RG = OFF_RV + RET_V
OFF_SQ = OFF_RG + RET_V
OFF_SK = OFF_SQ + SB_W
OFF_SV = OFF_SK + SB_W
OFF_GR = OFF_SV + SB_W
OFF_GS = OFF_GR + D_MODEL

LANES = 128
PROJ_TN = 512
PROJ_ROWS = 256
FFN_TM = 2048
FFN_TF = 256
FFN_ROWS = 256
MERGE_TM = 1024
MERGE_ROWS = 256
RET_UNROLL = 16
VMEM_LIMIT = 56 * 1024 * 1024

_NT = (((1,), (1,)), ((), ()))
_TN = (((0,), (0,)), ((), ()))


def _sigmoid(x):
    return 1.0 / (1.0 + jnp.exp(-x))


def _rmsnorm_rows(x, g):
    ms = jnp.mean(x * x, axis=-1, keepdims=True)
    return x * lax.rsqrt(ms + NORM_EPS) * g


def _inproj_kernel(h_ref, g_ref, w_ref, cos_ref, sin_ref, o_ref, hn_ref):
    j = pl.program_id(1)
    S = hn_ref.shape[0]

    @pl.when(j == 0)
    def _():
        hn_ref[...] = _rmsnorm_rows(h_ref[...], g_ref[...]).astype(BF16)

    def project(epilogue):
        w = w_ref[...].astype(BF16)
        for r in range(0, S, PROJ_ROWS):
            rows = slice(r, r + PROJ_ROWS)
            y = jnp.dot(hn_ref[rows, :], w, preferred_element_type=F32)
            epilogue(y, rows)

    j_rk = OFF_RK // PROJ_TN
    j_rv = OFF_RV // PROJ_TN
    j_rg = OFF_RG // PROJ_TN
    j_sq = OFF_SQ // PROJ_TN
    j_sk = OFF_SK // PROJ_TN
    j_gr = OFF_GR // PROJ_TN

    @pl.when(j < j_rv)
    def _():
        scale = jnp.where(j == j_rk, RET_DK ** -0.5, 1.0).astype(F32)

        def rope(y, rows):
            cos = cos_ref[rows, :]
            sin = sin_ref[rows, :]
            for hd in range(PROJ_TN // RET_DK):
                sl = slice(hd * RET_DK, (hd + 1) * RET_DK)
                yh = y[:, sl]
                r = yh * cos + pltpu.roll(yh, RET_DK // 2, 1) * sin
                o_ref[rows, sl] = (r * scale).astype(BF16)

        project(rope)

    @pl.when(((j >= j_rv) & (j < j_rg)) | ((j >= j_sk) & (j < j_gr)))
    def _():
        def identity(y, rows):
            o_ref[rows, :] = y.astype(BF16)

        project(identity)

    @pl.when((j >= j_rg) & (j < j_sq))
    def _():
        def silu(y, rows):
            o_ref[rows, :] = (y * _sigmoid(y)).astype(BF16)

        project(silu)

    @pl.when(j == j_sq)
    def _():
        def scaled(y, rows):
            o_ref[rows, :] = (y * (SB_DH ** -0.5)).astype(BF16)

        project(scaled)

    @pl.when(j >= j_gr)
    def _():
        def gate(y, rows):
            o_ref[rows, :] = _sigmoid(y).astype(BF16)

        project(gate)


def _inproj(h, g, w_in, layer, cos, sin):
    B, S, D = h.shape
    return pl.pallas_call(
        _inproj_kernel,
        out_shape=jax.ShapeDtypeStruct((B, S, IN_COLS), BF16),
        grid=(B, IN_COLS // PROJ_TN),
        in_specs=[
            pl.BlockSpec((None, S, D), lambda b, j: (b, 0, 0)),
            pl.BlockSpec((1, D), lambda b, j: (0, 0)),
            pl.BlockSpec((None, D, PROJ_TN), lambda b, j: (layer, 0, j)),
            pl.BlockSpec((S, RET_DK), lambda b, j: (0, 0)),
            pl.BlockSpec((S, RET_DK), lambda b, j: (0, 0)),
        ],
        out_specs=pl.BlockSpec((None, S, PROJ_TN), lambda b, j: (b, 0, j)),
        scratch_shapes=[pltpu.VMEM((S, D), BF16)],
        compiler_params=pltpu.CompilerParams(
            dimension_semantics=("parallel", "arbitrary"), vmem_limit_bytes=VMEM_LIMIT),
        name="inproj",
    )(h, g, w_in, cos, sin)


def _retention_kernel(lg_ref, q_ref, k_ref, v_ref, g_ref, o_ref, kv_ref, r_ref):
    C = RET_CHUNK
    n_chunks = q_ref.shape[0] // C
    lg_row = lg_ref[...]
    lg = lg_row[:, :1]
    ii = lax.broadcasted_iota(jnp.int32, (C, C), 0)
    jj = lax.broadcasted_iota(jnp.int32, (C, C), 1)
    diff = (ii - jj).astype(F32)
    dec = jnp.where(diff >= 0, jnp.exp(lg_row * jnp.maximum(diff, 0.0)), 0.0)
    i_col = lax.broadcasted_iota(jnp.int32, (C, 1), 0).astype(F32)
    k_dec = jnp.exp(lg * (C - 1.0 - i_col))
    q_dec = jnp.exp(lg * (i_col + 1.0))
    chunk_dec = jnp.exp(lg * C)

    def chunk_rows(n):
        return pl.ds(pl.multiple_of(n * C, C), C)

    def kv_body(n, carry):
        rows = chunk_rows(n)
        kd = (k_ref[rows, :].astype(F32) * k_dec).astype(BF16)
        kv_ref[n] = lax.dot_general(kd, v_ref[rows, :], _TN, preferred_element_type=F32)
        return carry

    lax.fori_loop(0, n_chunks, kv_body, 0, unroll=RET_UNROLL)

    def scan_body(n, r):
        r_ref[n] = r.astype(BF16)
        return chunk_dec * r + kv_ref[n]

    lax.fori_loop(0, n_chunks, scan_body, jnp.zeros((RET_DK, RET_DV), F32), unroll=RET_UNROLL)

    def out_body(n, carry):
        rows = chunk_rows(n)
        q = q_ref[rows, :]
        v = v_ref[rows, :]
        s = lax.dot_general(q, k_ref[rows, :], _NT, preferred_element_type=F32) * dec
        lhs = jnp.concatenate([s.astype(BF16), (q.astype(F32) * q_dec).astype(BF16)], axis=1)
        rhs = jnp.concatenate([v, r_ref[n]], axis=0)
        o = jnp.dot(lhs, rhs, preferred_element_type=F32)
        mu = jnp.mean(o, axis=-1, keepdims=True)
        yc = o - mu
        var = jnp.mean(yc * yc, axis=-1, keepdims=True)
        y = yc * lax.rsqrt(var + NORM_EPS)
        o_ref[rows, :] = (y * g_ref[rows, :].astype(F32)).astype(BF16)
        return carry

    lax.fori_loop(0, n_chunks, out_body, 0, unroll=RET_UNROLL)


def _retention(proj, log_g):
    B, S, _ = proj.shape
    qb, kb = OFF_RQ // RET_DK, OFF_RK // RET_DK
    vb, gb = OFF_RV // RET_DV, OFF_RG // RET_DV
    return pl.pallas_call(
        _retention_kernel,
        out_shape=jax.ShapeDtypeStruct((B, S, RET_V), BF16),
        grid=(B, RET_HEADS),
        in_specs=[
            pl.BlockSpec((None, 1, LANES), lambda b, h: (h, 0, 0)),
            pl.BlockSpec((None, S, RET_DK), lambda b, h: (b, 0, qb + h)),
            pl.BlockSpec((None, S, RET_DK), lambda b, h: (b, 0, kb + h)),
            pl.BlockSpec((None, S, RET_DV), lambda b, h: (b, 0, vb + h)),
            pl.BlockSpec((None, S, RET_DV), lambda b, h: (b, 0, gb + h)),
        ],
        out_specs=pl.BlockSpec((None, S, RET_DV), lambda b, h: (b, 0, h)),
        scratch_shapes=[pltpu.VMEM((S // RET_CHUNK, RET_DK, RET_DV), F32),
                        pltpu.VMEM((S // RET_CHUNK, RET_DK, RET_DV), BF16)],
        compiler_params=pltpu.CompilerParams(
            dimension_semantics=("parallel", "parallel"), vmem_limit_bytes=VMEM_LIMIT),
        name="retention",
    )(log_g, proj, proj, proj, proj)


SB_PAIRS = SB_W // LANES
SB_STAGES = 5
SB_Z_SLOTS = 4
SB_UNROLL = 4
SB_DIAG_GROUP = 4
SOFTPLUS_CAP = 80.0


def _sb_schedule(n_blocks):
    tiles = [(qi, qi - s) for qi in range(1, n_blocks) for s in range(1, qi + 1)]
    pad = SB_STAGES - 1
    n_steps = -(-(len(tiles) + pad) // SB_UNROLL) * SB_UNROLL
    tab = np.zeros((3, n_steps + pad), np.int32)
    tab[1, :] = n_blocks
    for t, (qi, j) in enumerate(tiles):
        tab[:, t + pad] = (qi, qi, j)
    return tab


def _sb_kernel(tab_ref, q_ref, k_ref, v_ref, t_ref, o_ref,
               q2_ref, v2_ref, carry_ref, acc_ref, z_ref, sp_ref, cs_ref, a_ref):
    BLK = SB_BLOCK
    n_blocks = q_ref.shape[0] // BLK
    pad = SB_STAGES - 1
    n_steps = tab_ref.shape[1] - pad
    pairs = range(SB_PAIRS)
    cols = [slice(p * LANES, (p + 1) * LANES) for p in pairs]
    lane = lax.broadcasted_iota(jnp.int32, (BLK, LANES), 1)
    head0 = lane < SB_DH
    qi_loc = lax.broadcasted_iota(jnp.int32, (2 * BLK, BLK), 0) % BLK
    ki_loc = lax.broadcasted_iota(jnp.int32, (2 * BLK, BLK), 1)
    diag_mask = ki_loc < qi_loc

    def block_rows(j):
        return pl.ds(pl.multiple_of(j * BLK, BLK), BLK)

    def split_heads(x):
        zero = jnp.zeros_like(x)
        return jnp.concatenate([jnp.where(head0, x, zero), jnp.where(head0, zero, x)], axis=0)

    def softplus(z):
        return jnp.maximum(z, jnp.log(1.0 + jnp.exp(jnp.minimum(z, SOFTPLUS_CAP))))

    def stack_heads(a):
        ab = a.astype(BF16)
        return jnp.concatenate([ab[:BLK], ab[BLK:]], axis=1)

    def lane_bcast_col0(x):
        return jnp.broadcast_to(x[:, :1], x.shape)

    def prep(b, _):
        rows = block_rows(b)
        for p in pairs:
            q2_ref[b, p] = split_heads(q_ref[rows, cols[p]])
            v2_ref[b, p] = split_heads(v_ref[rows, cols[p]])
        return 0

    lax.fori_loop(0, n_blocks, prep, 0)
    z_ref[...] = jnp.zeros_like(z_ref)
    sp_ref[...] = jnp.zeros_like(sp_ref)
    cs_ref[...] = jnp.zeros_like(cs_ref)
    a_ref[...] = jnp.zeros_like(a_ref)
    carry_ref[n_blocks] = jnp.zeros(carry_ref.shape[1:], F32)
    acc_ref[n_blocks] = jnp.zeros(acc_ref.shape[1:], F32)

    def diag(i, _):
        work = [(SB_DIAG_GROUP * i + d, p) for d in range(SB_DIAG_GROUP) for p in pairs]
        zs = [lax.dot_general(q2_ref[qi, p], k_ref[block_rows(qi), cols[p]], _NT,
                              preferred_element_type=F32) for qi, p in work]
        sps = [jnp.where(diag_mask, softplus(z), 0.0).astype(BF16) for z in zs]
        css = [jnp.dot(sp, t_ref[...], preferred_element_type=F32) for sp in sps]
        a2s = []
        for (qi, p), z, suffix in zip(work, zs, css):
            a2s.append(stack_heads(jnp.where(diag_mask, jnp.exp(z - suffix), 0.0)))
            carry_ref[qi, p] = lane_bcast_col0(suffix)
        for (qi, p), a2 in zip(work, a2s):
            acc_ref[qi, p] = jnp.dot(a2, v2_ref[qi, p], preferred_element_type=F32)
        return 0

    lax.fori_loop(0, n_blocks // SB_DIAG_GROUP, diag, 0)

    def step(s, u):
        qi_qk, j_qk = tab_ref[0, s + pad], tab_ref[2, s + pad]
        slot_exp = tab_ref[1, s + pad - 3]
        slot_av, j_av = tab_ref[1, s], tab_ref[2, s]
        zslot_qk = u % SB_Z_SLOTS
        zslot_sp = (u - 1) % SB_Z_SLOTS
        zslot_exp = (u - 3) % SB_Z_SLOTS
        rows_qk = block_rows(j_qk)
        pvs = [jnp.dot(a_ref[p], v2_ref[j_av, p], preferred_element_type=F32) for p in pairs]
        css = [jnp.dot(sp_ref[p], t_ref[...], preferred_element_type=F32) for p in pairs]
        zs = [lax.dot_general(q2_ref[qi_qk, p], k_ref[rows_qk, cols[p]], _NT,
                              preferred_element_type=F32) for p in pairs]
        sps = [softplus(z_ref[zslot_sp, p]).astype(BF16) for p in pairs]
        a2s, carries = [], []
        for p in pairs:
            suffix = cs_ref[p] + carry_ref[slot_exp, p]
            a2s.append(stack_heads(jnp.exp(z_ref[zslot_exp, p] - suffix)))
            carries.append(lane_bcast_col0(suffix))
        for p in pairs:
            sp_ref[p] = sps[p]
            a_ref[p] = a2s[p]
            carry_ref[slot_exp, p] = carries[p]
            cs_ref[p] = css[p]
            z_ref[zslot_qk, p] = zs[p]
            acc_ref[slot_av, p] += pvs[p]

    def steps(i, _):
        for u in range(SB_UNROLL):
            step(i * SB_UNROLL + u, u)
        return 0

    lax.fori_loop(0, n_steps // SB_UNROLL, steps, 0)

    def emit(b, _):
        rows = block_rows(b)
        for p in pairs:
            o_ref[rows, cols[p]] = acc_ref[b, p].astype(BF16)
        return 0

    lax.fori_loop(0, n_blocks, emit, 0)


def _stick_breaking(proj, tri):
    B, S, _ = proj.shape
    n_blocks = S // SB_BLOCK
    assert n_blocks % SB_DIAG_GROUP == 0
    qb, kb, vb = OFF_SQ // SB_W, OFF_SK // SB_W, OFF_SV // SB_W
    tab = jnp.asarray(_sb_schedule(n_blocks))
    stacked = (SB_PAIRS, 2 * SB_BLOCK, SB_BLOCK)
    grid_spec = pltpu.PrefetchScalarGridSpec(
        num_scalar_prefetch=1,
        grid=(B,),
        in_specs=[
            pl.BlockSpec((None, S, SB_W), lambda b, tab: (b, 0, qb)),
            pl.BlockSpec((None, S, SB_W), lambda b, tab: (b, 0, kb)),
            pl.BlockSpec((None, S, SB_W), lambda b, tab: (b, 0, vb)),
            pl.BlockSpec((SB_BLOCK, SB_BLOCK), lambda b, tab: (0, 0)),
        ],
        out_specs=pl.BlockSpec((None, S, SB_W), lambda b, tab: (b, 0, 0)),
        scratch_shapes=[
            pltpu.VMEM((n_blocks,) + stacked, BF16),
            pltpu.VMEM((n_blocks,) + stacked, BF16),
            pltpu.VMEM((n_blocks + 1,) + stacked, F32),
            pltpu.VMEM((n_blocks + 1, SB_PAIRS, SB_BLOCK, LANES), F32),
            pltpu.VMEM((SB_Z_SLOTS,) + stacked, F32),
            pltpu.VMEM(stacked, BF16),
            pltpu.VMEM(stacked, F32),
            pltpu.VMEM((SB_PAIRS, SB_BLOCK, 2 * SB_BLOCK), BF16),
        ],
    )
    return pl.pallas_call(
        _sb_kernel,
        out_shape=jax.ShapeDtypeStruct((B, S, SB_W), BF16),
        grid_spec=grid_spec,
        compiler_params=pltpu.CompilerParams(
            dimension_semantics=("parallel",), vmem_limit_bytes=VMEM_LIMIT),
        name="stick_breaking",
    )(tab, proj, proj, proj, tri)


def _merge_kernel(h_ref, ret_ref, sb_ref, gr0_ref, gr1_ref, gs0_ref, gs1_ref,
                  wr_ref, ws_ref, wo_ref, o_ref):
    wr = wr_ref[...].astype(BF16)
    ws = ws_ref[...].astype(BF16)
    wo = wo_ref[...].astype(BF16)

    def branches(rows):
        ro = jnp.dot(ret_ref[rows, :], wr, preferred_element_type=F32)
        so = jnp.dot(sb_ref[rows, :], ws, preferred_element_type=F32)
        gr = jnp.concatenate([gr0_ref[rows, :], gr1_ref[rows, :]], axis=1).astype(F32)
        gs = jnp.concatenate([gs0_ref[rows, :], gs1_ref[rows, :]], axis=1).astype(F32)
        return (gr * ro + gs * so).astype(BF16)

    def project(rows, merged):
        o_ref[rows, :] = h_ref[rows, :] + jnp.dot(merged, wo, preferred_element_type=F32)

    chunks = [slice(r, r + MERGE_ROWS) for r in range(0, o_ref.shape[0], MERGE_ROWS)]
    prev = None
    for rows in chunks:
        merged = branches(rows)
        if prev is not None:
            project(*prev)
        prev = (rows, merged)
    project(*prev)


def _merge(h, ret, sb, proj, w_ret_o, w_sb_o, w_out, layer):
    B, S, D = h.shape
    tm = min(MERGE_TM, S)
    half = D_MODEL // 2
    gr0, gs0 = OFF_GR // half, OFF_GS // half
    tok = lambda width, col: pl.BlockSpec((None, tm, width), lambda b, i: (b, i, col))
    full = lambda a: pl.BlockSpec((None,) + a.shape[1:], lambda b, i: (layer, 0, 0),
                                  pipeline_mode=pl.Buffered(1))
    return pl.pallas_call(
        _merge_kernel,
        out_shape=jax.ShapeDtypeStruct((B, S, D), F32),
        grid=(B, S // tm),
        in_specs=[tok(D, 0), tok(RET_V, 0), tok(SB_W, 0),
                  tok(half, gr0), tok(half, gr0 + 1), tok(half, gs0), tok(half, gs0 + 1),
                  full(w_ret_o), full(w_sb_o), full(w_out)],
        out_specs=tok(D, 0),
        compiler_params=pltpu.CompilerParams(
            dimension_semantics=("parallel", "parallel"), vmem_limit_bytes=VMEM_LIMIT),
        name="merge",
    )(h, ret, sb, proj, proj, proj, proj, w_ret_o, w_sb_o, w_out)


def _ffn_kernel(h_ref, g_ref, wg_ref, wu_ref, wd_ref, gf_ref, o_ref, hn_ref, *, final):
    f = pl.program_id(1)

    @pl.when(f == 0)
    def _():
        x = h_ref[...]
        hn_ref[...] = _rmsnorm_rows(x, g_ref[...]).astype(BF16)
        o_ref[...] = x

    wg = wg_ref[...].astype(BF16)
    wu = wu_ref[...].astype(BF16)
    wd = wd_ref[...].astype(BF16)

    def up(rows):
        hn = hn_ref[rows, :]
        g = jnp.dot(hn, wg, preferred_element_type=F32)
        u = jnp.dot(hn, wu, preferred_element_type=F32)
        return (g * _sigmoid(g) * u).astype(BF16)

    def down(rows, act):
        o_ref[rows, :] += jnp.dot(act, wd, preferred_element_type=F32)

    chunks = [slice(r, r + FFN_ROWS) for r in range(0, hn_ref.shape[0], FFN_ROWS)]
    prev = None
    for rows in chunks:
        act = up(rows)
        if prev is not None:
            down(*prev)
        prev = (rows, act)
    down(*prev)

    if final:
        @pl.when(f == pl.num_programs(1) - 1)
        def _():
            o_ref[...] = _rmsnorm_rows(o_ref[...], gf_ref[...])


def _ffn(h, g, w_gate_up, w_down, layer, g_final, final):
    B, S, D = h.shape
    T = B * S
    tm = min(FFN_TM, T)
    h2 = h.reshape(T, D)
    nf = FF // FFN_TF
    out = pl.pallas_call(
        functools.partial(_ffn_kernel, final=final),
        out_shape=jax.ShapeDtypeStruct((T, D), F32),
        grid=(T // tm, nf),
        in_specs=[
            pl.BlockSpec((tm, D), lambda i, f: (i, 0)),
            pl.BlockSpec((1, D), lambda i, f: (0, 0)),
            pl.BlockSpec((None, D, FFN_TF), lambda i, f: (layer, 0, f)),
            pl.BlockSpec((None, D, FFN_TF), lambda i, f: (layer, 0, nf + f)),
            pl.BlockSpec((None, FFN_TF, D), lambda i, f: (layer, f, 0)),
            pl.BlockSpec((1, D), lambda i, f: (0, 0)),
        ],
        out_specs=pl.BlockSpec((tm, D), lambda i, f: (i, 0)),
        scratch_shapes=[pltpu.VMEM((tm, D), BF16)],
        compiler_params=pltpu.CompilerParams(
            dimension_semantics=("parallel", "arbitrary"), vmem_limit_bytes=VMEM_LIMIT),
        name="ffn_final" if final else "ffn",
    )(h2, g, w_gate_up, w_gate_up, w_down, g_final)
    return out.reshape(B, S, D)


def _rope_tables(S):
    half = RET_DK // 2
    pos = jnp.arange(S, dtype=F32)
    inv = 1.0 / (ROPE_BASE ** (jnp.arange(half, dtype=F32) / half))
    ang = pos[:, None] * inv[None, :]
    cos = jnp.cos(ang)
    sin = jnp.sin(ang)
    return jnp.concatenate([cos, cos], axis=-1), jnp.concatenate([-sin, sin], axis=-1)


def kernel(x, norm_mix, w_in, w_ret_o, w_sb_o, w_out, norm_ffn, w_gate_up, w_down, norm_final):
    B, S, D = x.shape
    depth = w_in.shape[0]
    cos, sin = _rope_tables(S)
    log_g = jnp.log1p(-jnp.exp2(-5.0 - jnp.arange(RET_HEADS, dtype=F32)))
    log_g = jnp.broadcast_to(log_g[:, None, None], (RET_HEADS, 1, LANES))
    r = jnp.arange(SB_BLOCK)[:, None]
    c = jnp.arange(SB_BLOCK)[None, :]
    tri = (r >= c).astype(BF16)
    g_final = norm_final.reshape(1, D)

    h = x
    for layer in range(depth):
        proj = _inproj(h, norm_mix[layer].reshape(1, D), w_in, layer, cos, sin)
        ret = _retention(proj, log_g)
        sb = _stick_breaking(proj, tri)
        h = _merge(h, ret, sb, proj, w_ret_o, w_sb_o, w_out, layer)
        h = _ffn(h, norm_ffn[layer].reshape(1, D), w_gate_up, w_down, layer, g_final,
                 final=(layer == depth - 1))
    return h
```

```python
import functools

import numpy as np
import jax
import jax.numpy as jnp
from jax import lax
from jax.experimental import pallas as pl
from jax.experimental.pallas import tpu as pltpu

F32 = jnp.float32
BF16 = jnp.bfloat16

D_MODEL = 1024
RET_HEADS = 4
RET_DK = 128
RET_DV = 256
RET_QK = RET_HEADS * RET_DK
RET_V = RET_HEADS * RET_DV
RET_CHUNK = 128
ROPE_BASE = 10000.0
SB_HEADS = 8
SB_DH = 64
SB_W = SB_HEADS * SB_DH
SB_BLOCK = 128
FF = 2816
NORM_EPS = 1e-6
IN_COLS = 2 * RET_QK + 2 * RET_V + 3 * SB_W + 2 * D_MODEL

OFF_RQ = 0
OFF_RK = OFF_RQ + RET_QK
OFF_RV = OFF_RK + RET_QK
OFF_RG = OFF_RV + RET_V
OFF_SQ = OFF_RG + RET_V
OFF_SK = OFF_SQ + SB_W
OFF_SV = OFF_SK + SB_W
OFF_GR = OFF_SV + SB_W
OFF_GS = OFF_GR + D_MODEL

LANES = 128
PROJ_TN = 512
PROJ_ROWS = 256
FFN_TM = 2048
FFN_TF = 256
FFN_ROWS = 256
MERGE_TM = 1024
MERGE_ROWS = 256
RET_UNROLL = 16
VMEM_LIMIT = 56 * 1024 * 1024

_NT = (((1,), (1,)), ((), ()))
_TN = (((0,), (0,)), ((), ()))


def _sigmoid(x):
    return 1.0 / (1.0 + jnp.exp(-x))


def _rmsnorm_rows(x, g):
    ms = jnp.mean(x * x, axis=-1, keepdims=True)
    return x * lax.rsqrt(ms + NORM_EPS) * g


def _inproj_kernel(h_ref, g_ref, w_ref, cos_ref, sin_ref, o_ref, hn_ref):
    j = pl.program_id(1)
    S = hn_ref.shape[0]

    @pl.when(j == 0)
    def _():
        hn_ref[...] = _rmsnorm_rows(h_ref[...], g_ref[...]).astype(BF16)

    def project(epilogue):
        w = w_ref[...].astype(BF16)
        for r in range(0, S, PROJ_ROWS):
            rows = slice(r, r + PROJ_ROWS)
            y = jnp.dot(hn_ref[rows, :], w, preferred_element_type=F32)
            epilogue(y, rows)

    j_rk = OFF_RK // PROJ_TN
    j_rv = OFF_RV // PROJ_TN
    j_rg = OFF_RG // PROJ_TN
    j_sq = OFF_SQ // PROJ_TN
    j_sk = OFF_SK // PROJ_TN
    j_gr = OFF_GR // PROJ_TN

    @pl.when(j < j_rv)
    def _():
        scale = jnp.where(j == j_rk, RET_DK ** -0.5, 1.0).astype(F32)

        def rope(y, rows):
            cos = cos_ref[rows, :]
            sin = sin_ref[rows, :]
            for hd in range(PROJ_TN // RET_DK):
                sl = slice(hd * RET_DK, (hd + 1) * RET_DK)
                yh = y[:, sl]
                r = yh * cos + pltpu.roll(yh, RET_DK // 2, 1) * sin
                o_ref[rows, sl] = (r * scale).astype(BF16)

        project(rope)

    @pl.when(((j >= j_rv) & (j < j_rg)) | ((j >= j_sk) & (j < j_gr)))
    def _():
        def identity(y, rows):
            o_ref[rows, :] = y.astype(BF16)

        project(identity)

    @pl.when((j >= j_rg) & (j < j_sq))
    def _():
        def silu(y, rows):
            o_ref[rows, :] = (y * _sigmoid(y)).astype(BF16)

        project(silu)

    @pl.when(j == j_sq)
    def _():
        def scaled(y, rows):
            o_ref[rows, :] = (y * (SB_DH ** -0.5)).astype(BF16)

        project(scaled)

    @pl.when(j >= j_gr)
    def _():
        def gate(y, rows):
            o_ref[rows, :] = _sigmoid(y).astype(BF16)

        project(gate)


def _inproj(h, g, w_in, layer, cos, sin):
    B, S, D = h.shape
    return pl.pallas_call(
        _inproj_kernel,
        out_shape=jax.ShapeDtypeStruct((B, S, IN_COLS), BF16),
        grid=(B, IN_COLS // PROJ_TN),
        in_specs=[
            pl.BlockSpec((None, S, D), lambda b, j: (b, 0, 0)),
            pl.BlockSpec((1, D), lambda b, j: (0, 0)),
            pl.BlockSpec((None, D, PROJ_TN), lambda b, j: (layer, 0, j)),
            pl.BlockSpec((S, RET_DK), lambda b, j: (0, 0)),
            pl.BlockSpec((S, RET_DK), lambda b, j: (0, 0)),
        ],
        out_specs=pl.BlockSpec((None, S, PROJ_TN), lambda b, j: (b, 0, j)),
        scratch_shapes=[pltpu.VMEM((S, D), BF16)],
        compiler_params=pltpu.CompilerParams(
            dimension_semantics=("parallel", "arbitrary"), vmem_limit_bytes=VMEM_LIMIT),
        name="inproj",
    )(h, g, w_in, cos, sin)


def _retention_kernel(lg_ref, q_ref, k_ref, v_ref, g_ref, o_ref, kv_ref, r_ref):
    C = RET_CHUNK
    n_chunks = q_ref.shape[0] // C
    lg_row = lg_ref[...]
    lg = lg_row[:, :1]
    ii = lax.broadcasted_iota(jnp.int32, (C, C), 0)
    jj = lax.broadcasted_iota(jnp.int32, (C, C), 1)
    diff = (ii - jj).astype(F32)
    dec = jnp.where(diff >= 0, jnp.exp(lg_row * jnp.maximum(diff, 0.0)), 0.0)
    i_col = lax.broadcasted_iota(jnp.int32, (C, 1), 0).astype(F32)
    k_dec = jnp.exp(lg * (C - 1.0 - i_col))
    q_dec = jnp.exp(lg * (i_col + 1.0))
    chunk_dec = jnp.exp(lg * C)

    def chunk_rows(n):
        return pl.ds(pl.multiple_of(n * C, C), C)

    def kv_body(n, carry):
        rows = chunk_rows(n)
        kd = (k_ref[rows, :].astype(F32) * k_dec).astype(BF16)
        kv_ref[n] = lax.dot_general(kd, v_ref[rows, :], _TN, preferred_element_type=F32)
        return carry

    lax.fori_loop(0, n_chunks, kv_body, 0, unroll=RET_UNROLL)

    def scan_body(n, r):
        r_ref[n] = r.astype(BF16)
        return chunk_dec * r + kv_ref[n]

    lax.fori_loop(0, n_chunks, scan_body, jnp.zeros((RET_DK, RET_DV), F32), unroll=RET_UNROLL)

    def out_body(n, carry):
        rows = chunk_rows(n)
        q = q_ref[rows, :]
        v = v_ref[rows, :]
        s = lax.dot_general(q, k_ref[rows, :], _NT, preferred_element_type=F32) * dec
        lhs = jnp.concatenate([s.astype(BF16), (q.astype(F32) * q_dec).astype(BF16)], axis=1)
        rhs = jnp.concatenate([v, r_ref[n]], axis=0)
        o = jnp.dot(lhs, rhs, preferred_element_type=F32)
        mu = jnp.mean(o, axis=-1, keepdims=True)
        yc = o - mu
        var = jnp.mean(yc * yc, axis=-1, keepdims=True)
        y = yc * lax.rsqrt(var + NORM_EPS)
        o_ref[rows, :] = (y * g_ref[rows, :].astype(F32)).astype(BF16)
        return carry

    lax.fori_loop(0, n_chunks, out_body, 0, unroll=RET_UNROLL)


def _retention(proj, log_g):
    B, S, _ = proj.shape
    qb, kb = OFF_RQ // RET_DK, OFF_RK // RET_DK
    vb, gb = OFF_RV // RET_DV, OFF_RG // RET_DV
    return pl.pallas_call(
        _retention_kernel,
        out_shape=jax.ShapeDtypeStruct((B, S, RET_V), BF16),
        grid=(B, RET_HEADS),
        in_specs=[
            pl.BlockSpec((None, 1, LANES), lambda b, h: (h, 0, 0)),
            pl.BlockSpec((None, S, RET_DK), lambda b, h: (b, 0, qb + h)),
            pl.BlockSpec((None, S, RET_DK), lambda b, h: (b, 0, kb + h)),
            pl.BlockSpec((None, S, RET_DV), lambda b, h: (b, 0, vb + h)),
            pl.BlockSpec((None, S, RET_DV), lambda b, h: (b, 0, gb + h)),
        ],
        out_specs=pl.BlockSpec((None, S, RET_DV), lambda b, h: (b, 0, h)),
        scratch_shapes=[pltpu.VMEM((S // RET_CHUNK, RET_DK, RET_DV), F32),
                        pltpu.VMEM((S // RET_CHUNK, RET_DK, RET_DV), BF16)],
        compiler_params=pltpu.CompilerParams(
            dimension_semantics=("parallel", "parallel"), vmem_limit_bytes=VMEM_LIMIT),
        name="retention",
    )(log_g, proj, proj, proj, proj)


SB_PAIRS = SB_W // LANES
SB_STAGES = 5
SB_Z_SLOTS = 4
SB_UNROLL = 32
SB_DIAG_GROUP = 8
SOFTPLUS_CAP = 80.0


def _sb_schedule(n_blocks):
    tiles = [(qi, qi - s) for qi in range(1, n_blocks) for s in range(1, qi + 1)]
    pad = SB_STAGES - 1
    n_steps = -(-(len(tiles) + pad) // SB_UNROLL) * SB_UNROLL
    tab = np.zeros((3, n_steps + pad), np.int32)
    tab[1, :] = n_blocks
    for t, (qi, j) in enumerate(tiles):
        tab[:, t + pad] = (qi, qi, j)
    return tab


def _sb_kernel(tab_ref, q_ref, k_ref, v_ref, t_ref, o_ref,
               q2_ref, v2_ref, carry_ref, acc_ref, z_ref, sp_ref, cs_ref, a_ref):
    BLK = SB_BLOCK
    n_blocks = q_ref.shape[0] // BLK
    pad = SB_STAGES - 1
    n_steps = tab_ref.shape[1] - pad
    pairs = range(SB_PAIRS)
    cols = [slice(p * LANES, (p + 1) * LANES) for p in pairs]
    lane = lax.broadcasted_iota(jnp.int32, (BLK, LANES), 1)
    head0 = lane < SB_DH
    qi_loc = lax.broadcasted_iota(jnp.int32, (2 * BLK, BLK), 0) % BLK
    ki_loc = lax.broadcasted_iota(jnp.int32, (2 * BLK, BLK), 1)
    diag_mask = ki_loc < qi_loc

    def block_rows(j):
        return pl.ds(pl.multiple_of(j * BLK, BLK), BLK)

    def split_heads(x):
        zero = jnp.zeros_like(x)
        return jnp.concatenate([jnp.where(head0, x, zero), jnp.where(head0, zero, x)], axis=0)

    def softplus(z):
        return jnp.maximum(z, jnp.log(1.0 + jnp.exp(jnp.minimum(z, SOFTPLUS_CAP))))

    def stack_heads(a):
        ab = a.astype(BF16)
        return jnp.concatenate([ab[:BLK], ab[BLK:]], axis=1)

    def lane_bcast_col0(x):
        return jnp.broadcast_to(x[:, :1], x.shape)

    def prep(b, _):
        rows = block_rows(b)
        for p in pairs:
            q2_ref[b, p] = split_heads(q_ref[rows, cols[p]])
            v2_ref[b, p] = split_heads(v_ref[rows, cols[p]])
        return 0

    lax.fori_loop(0, n_blocks, prep, 0)
    z_ref[...] = jnp.zeros_like(z_ref)
    sp_ref[...] = jnp.zeros_like(sp_ref)
    cs_ref[...] = jnp.zeros_like(cs_ref)
    a_ref[...] = jnp.zeros_like(a_ref)
    carry_ref[n_blocks] = jnp.zeros(carry_ref.shape[1:], F32)
    acc_ref[n_blocks] = jnp.zeros(acc_ref.shape[1:], F32)

    def diag(i, _):
        work = [(SB_DIAG_GROUP * i + d, p) for d in range(SB_DIAG_GROUP) for p in pairs]
        zs = [lax.dot_general(q2_ref[qi, p], k_ref[block_rows(qi), cols[p]], _NT,
                              preferred_element_type=F32) for qi, p in work]
        sps = [jnp.where(diag_mask, softplus(z), 0.0).astype(BF16) for z in zs]
        css = [jnp.dot(sp, t_ref[...], preferred_element_type=F32) for sp in sps]
        a2s = []
        for (qi, p), z, suffix in zip(work, zs, css):
            a2s.append(stack_heads(jnp.where(diag_mask, jnp.exp(z - suffix), 0.0)))
            carry_ref[qi, p] = lane_bcast_col0(suffix)
        for (qi, p), a2 in zip(work, a2s):
            acc_ref[qi, p] = jnp.dot(a2, v2_ref[qi, p], preferred_element_type=F32)
        return 0

    lax.fori_loop(0, n_blocks // SB_DIAG_GROUP, diag, 0)

    def step(s, u):
        qi_qk, j_qk = tab_ref[0, s + pad], tab_ref[2, s + pad]
        slot_exp = tab_ref[1, s + pad - 3]
        slot_av, j_av = tab_ref[1, s], tab_ref[2, s]
        zslot_qk = u % SB_Z_SLOTS
        zslot_sp = (u - 1) % SB_Z_SLOTS
        zslot_exp = (u - 3) % SB_Z_SLOTS
        rows_qk = block_rows(j_qk)
        pvs = [jnp.dot(a_ref[p], v2_ref[j_av, p], preferred_element_type=F32) for p in pairs]
        css = [jnp.dot(sp_ref[p], t_ref[...], preferred_element_type=F32) for p in pairs]
        zs = [lax.dot_general(q2_ref[qi_qk, p], k_ref[rows_qk, cols[p]], _NT,
                              preferred_element_type=F32) for p in pairs]
        sps = [softplus(z_ref[zslot_sp, p]).astype(BF16) for p in pairs]
        a2s, carries = [], []
        for p in pairs:
            suffix = cs_ref[p] + carry_ref[slot_exp, p]
            a2s.append(stack_heads(jnp.exp(z_ref[zslot_exp, p] - suffix)))
            carries.append(lane_bcast_col0(suffix))
        for p in pairs:
            sp_ref[p] = sps[p]
            a_ref[p] = a2s[p]
            carry_ref[slot_exp, p] = carries[p]
            cs_ref[p] = css[p]
            z_ref[zslot_qk, p] = zs[p]
            acc_ref[slot_av, p] += pvs[p]

    def steps(i, _):
        for u in range(SB_UNROLL):
            step(i * SB_UNROLL + u, u)
        return 0

    lax.fori_loop(0, n_steps // SB_UNROLL, steps, 0)

    def emit(b, _):
        rows = block_rows(b)
        for p in pairs:
            o_ref[rows, cols[p]] = acc_ref[b, p].astype(BF16)
        return 0

    lax.fori_loop(0, n_blocks, emit, 0)


def _stick_breaking(proj, tri):
    B, S, _ = proj.shape
    n_blocks = S // SB_BLOCK
    assert n_blocks % SB_DIAG_GROUP == 0
    qb, kb, vb = OFF_SQ // SB_W, OFF_SK // SB_W, OFF_SV // SB_W
    tab = jnp.asarray(_sb_schedule(n_blocks))
    stacked = (SB_PAIRS, 2 * SB_BLOCK, SB_BLOCK)
    grid_spec = pltpu.PrefetchScalarGridSpec(
        num_scalar_prefetch=1,
        grid=(B,),
        in_specs=[
            pl.BlockSpec((None, S, SB_W), lambda b, tab: (b, 0, qb)),
            pl.BlockSpec((None, S, SB_W), lambda b, tab: (b, 0, kb)),
            pl.BlockSpec((None, S, SB_W), lambda b, tab: (b, 0, vb)),
            pl.BlockSpec((SB_BLOCK, SB_BLOCK), lambda b, tab: (0, 0)),
        ],
        out_specs=pl.BlockSpec((None, S, SB_W), lambda b, tab: (b, 0, 0)),
        scratch_shapes=[
            pltpu.VMEM((n_blocks,) + stacked, BF16),
            pltpu.VMEM((n_blocks,) + stacked, BF16),
            pltpu.VMEM((n_blocks + 1,) + stacked, F32),
            pltpu.VMEM((n_blocks + 1, SB_PAIRS, SB_BLOCK, LANES), F32),
            pltpu.VMEM((SB_Z_SLOTS,) + stacked, F32),
            pltpu.VMEM(stacked, BF16),
            pltpu.VMEM(stacked, F32),
            pltpu.VMEM((SB_PAIRS, SB_BLOCK, 2 * SB_BLOCK), BF16),
        ],
    )
    return pl.pallas_call(
        _sb_kernel,
        out_shape=jax.ShapeDtypeStruct((B, S, SB_W), BF16),
        grid_spec=grid_spec,
        compiler_params=pltpu.CompilerParams(
            dimension_semantics=("parallel",), vmem_limit_bytes=VMEM_LIMIT),
        name="stick_breaking",
    )(tab, proj, proj, proj, tri)


def _merge_kernel(h_ref, ret_ref, sb_ref, gr0_ref, gr1_ref, gs0_ref, gs1_ref,
                  wr_ref, ws_ref, wo_ref, o_ref):
    wr = wr_ref[...].astype(BF16)
    ws = ws_ref[...].astype(BF16)
    wo = wo_ref[...].astype(BF16)

    def branches(rows):
        ro = jnp.dot(ret_ref[rows, :], wr, preferred_element_type=F32)
        so = jnp.dot(sb_ref[rows, :], ws, preferred_element_type=F32)
        gr = jnp.concatenate([gr0_ref[rows, :], gr1_ref[rows, :]], axis=1).astype(F32)
        gs = jnp.concatenate([gs0_ref[rows, :], gs1_ref[rows, :]], axis=1).astype(F32)
        return (gr * ro + gs * so).astype(BF16)

    def project(rows, merged):
        o_ref[rows, :] = h_ref[rows, :] + jnp.dot(merged, wo, preferred_element_type=F32)

    chunks = [slice(r, r + MERGE_ROWS) for r in range(0, o_ref.shape[0], MERGE_ROWS)]
    prev = None
    for rows in chunks:
        merged = branches(rows)
        if prev is not None:
            project(*prev)
        prev = (rows, merged)
    project(*prev)


def _merge(h, ret, sb, proj, w_ret_o, w_sb_o, w_out, layer):
    B, S, D = h.shape
    tm = min(MERGE_TM, S)
    half = D_MODEL // 2
    gr0, gs0 = OFF_GR // half, OFF_GS // half
    tok = lambda width, col: pl.BlockSpec((None, tm, width), lambda b, i: (b, i, col))
    full = lambda a: pl.BlockSpec((None,) + a.shape[1:], lambda b, i: (layer, 0, 0),
                                  pipeline_mode=pl.Buffered(1))
    return pl.pallas_call(
        _merge_kernel,
        out_shape=jax.ShapeDtypeStruct((B, S, D), F32),
        grid=(B, S // tm),
        in_specs=[tok(D, 0), tok(RET_V, 0), tok(SB_W, 0),
                  tok(half, gr0), tok(half, gr0 + 1), tok(half, gs0), tok(half, gs0 + 1),
                  full(w_ret_o), full(w_sb_o), full(w_out)],
        out_specs=tok(D, 0),
        compiler_params=pltpu.CompilerParams(
            dimension_semantics=("parallel", "parallel"), vmem_limit_bytes=VMEM_LIMIT),
        name="merge",
    )(h, ret, sb, proj, proj, proj, proj, w_ret_o, w_sb_o, w_out)


def _ffn_kernel(h_ref, g_ref, wg_ref, wu_ref, wd_ref, gf_ref, o_ref, hn_ref, *, final):
    f = pl.program_id(1)

    @pl.when(f == 0)
    def _():
        x = h_ref[...]
        hn_ref[...] = _rmsnorm_rows(x, g_ref[...]).astype(BF16)
        o_ref[...] = x

    wg = wg_ref[...].astype(BF16)
    wu = wu_ref[...].astype(BF16)
    wd = wd_ref[...].astype(BF16)

    def up(rows):
        hn = hn_ref[rows, :]
        g = jnp.dot(hn, wg, preferred_element_type=F32)
        u = jnp.dot(hn, wu, preferred_element_type=F32)
        return (g * _sigmoid(g) * u).astype(BF16)

    def down(rows, act):
        o_ref[rows, :] += jnp.dot(act, wd, preferred_element_type=F32)

    chunks = [slice(r, r + FFN_ROWS) for r in range(0, hn_ref.shape[0], FFN_ROWS)]
    prev = None
    for rows in chunks:
        act = up(rows)
        if prev is not None:
            down(*prev)
        prev = (rows, act)
    down(*prev)

    if final:
        @pl.when(f == pl.num_programs(1) - 1)
        def _():
            o_ref[...] = _rmsnorm_rows(o_ref[...], gf_ref[...])


def _ffn(h, g, w_gate_up, w_down, layer, g_final, final):
    B, S, D = h.shape
    T = B * S
    tm = min(FFN_TM, T)
    h2 = h.reshape(T, D)
    nf = FF // FFN_TF
    out = pl.pallas_call(
        functools.partial(_ffn_kernel, final=final),
        out_shape=jax.ShapeDtypeStruct((T, D), F32),
        grid=(T // tm, nf),
        in_specs=[
            pl.BlockSpec((tm, D), lambda i, f: (i, 0)),
            pl.BlockSpec((1, D), lambda i, f: (0, 0)),
            pl.BlockSpec((None, D, FFN_TF), lambda i, f: (layer, 0, f)),
            pl.BlockSpec((None, D, FFN_TF), lambda i, f: (layer, 0, nf + f)),
            pl.BlockSpec((None, FFN_TF, D), lambda i, f: (layer, f, 0)),
            pl.BlockSpec((1, D), lambda i, f: (0, 0)),
        ],
        out_specs=pl.BlockSpec((tm, D), lambda i, f: (i, 0)),
        scratch_shapes=[pltpu.VMEM((tm, D), BF16)],
        compiler_params=pltpu.CompilerParams(
            dimension_semantics=("parallel", "arbitrary"), vmem_limit_bytes=VMEM_LIMIT),
        name="ffn_final" if final else "ffn",
    )(h2, g, w_gate_up, w_gate_up, w_down, g_final)
    return out.reshape(B, S, D)


def _rope_tables(S):
    half = RET_DK // 2
    pos = jnp.arange(S, dtype=F32)
    inv = 1.0 / (ROPE_BASE ** (jnp.arange(half, dtype=F32) / half))
    ang = pos[:, None] * inv[None, :]
    cos = jnp.cos(ang)
    sin = jnp.sin(ang)
    return jnp.concatenate([cos, cos], axis=-1), jnp.concatenate([-sin, sin], axis=-1)


def kernel(x, norm_mix, w_in, w_ret_o, w_sb_o, w_out, norm_ffn, w_gate_up, w_down, norm_final):
    B, S, D = x.shape
    depth = w_in.shape[0]
    cos, sin = _rope_tables(S)
    log_g = jnp.log1p(-jnp.exp2(-5.0 - jnp.arange(RET_HEADS, dtype=F32)))
    log_g = jnp.broadcast_to(log_g[:, None, None], (RET_HEADS, 1, LANES))
    r = jnp.arange(SB_BLOCK)[:, None]
    c = jnp.arange(SB_BLOCK)[None, :]
    tri = (r >= c).astype(BF16)
    g_final = norm_final.reshape(1, D)

    h = x
    for layer in range(depth):
        proj = _inproj(h, norm_mix[layer].reshape(1, D), w_in, layer, cos, sin)
        ret = _retention(proj, log_g)
        sb = _stick_breaking(proj, tri)
        h = _merge(h, ret, sb, proj, w_ret_o, w_sb_o, w_out, layer)
        h = _ffn(h, norm_ffn[layer].reshape(1, D), w_gate_up, w_down, layer, g_final,
                 final=(layer == depth - 1))
    return h
```

```python
import functools

import numpy as np
import jax
import jax.numpy as jnp
from jax import lax
from jax.experimental import pallas as pl
from jax.experimental.pallas import tpu as pltpu

F32 = jnp.float32
BF16 = jnp.bfloat16

D_MODEL = 1024
RET_HEADS = 4
RET_DK = 128
RET_DV = 256
RET_QK = RET_HEADS * RET_DK
RET_V = RET_HEADS * RET_DV
RET_CHUNK = 128
ROPE_BASE = 10000.0
SB_HEADS = 8
SB_DH = 64
SB_W = SB_HEADS * SB_DH
SB_BLOCK = 128
FF = 2816
NORM_EPS = 1e-6
IN_COLS = 2 * RET_QK + 2 * RET_V + 3 * SB_W + 2 * D_MODEL

OFF_RQ = 0
OFF_RK = OFF_RQ + RET_QK
OFF_RV = OFF_RK + RET_QK
OFF_RG = OFF_RV + RET_V
OFF_SQ = OFF_RG + RET_V
OFF_SK = OFF_SQ + SB_W
OFF_SV = OFF_SK + SB_W
OFF_GR = OFF_SV + SB_W
OFF_GS = OFF_GR + D_MODEL

LANES = 128
PROJ_TM = 512
PROJ_TN = 512
PROJ_ROWS = 256
FFN_TM = 1024
FFN_ROWS = 256
MERGE_TM = 1024
MERGE_ROWS = 256
RET_UNROLL = 16
VMEM_LIMIT = 56 * 1024 * 1024

_NT = (((1,), (1,)), ((), ()))
_TN = (((0,), (0,)), ((), ()))


def _sigmoid(x):
    return 1.0 / (1.0 + jnp.exp(-x))


def _rmsnorm_rows(x, g):
    ms = jnp.mean(x * x, axis=-1, keepdims=True)
    return x * lax.rsqrt(ms + NORM_EPS) * g


def _inproj_kernel(h_ref, g_ref, w_ref, cos_ref, sin_ref, o_ref):
    def rope(scale):
        def epilogue(y, rows, cols):
            cos = cos_ref[rows, :]
            sin = sin_ref[rows, :]
            for c in range(cols.start, cols.stop, RET_DK):
                yh = y[:, c - cols.start:c - cols.start + RET_DK]
                r = yh * cos + pltpu.roll(yh, RET_DK // 2, 1) * sin
                if scale != 1.0:
                    r = r * scale
                o_ref[rows, c:c + RET_DK] = r.astype(BF16)
        return epilogue

    def elementwise(fn):
        def epilogue(y, rows, cols):
            o_ref[rows, cols] = fn(y).astype(BF16)
        return epilogue

    identity = elementwise(lambda y: y)
    gate = elementwise(_sigmoid)
    groups = [
        (OFF_RQ, RET_QK, rope(1.0)),
        (OFF_RK, RET_QK, rope(RET_DK ** -0.5)),
        (OFF_RV, RET_V, identity),
        (OFF_RG, RET_V, elementwise(lambda y: y * _sigmoid(y))),
        (OFF_SQ, SB_W, elementwise(lambda y: y * (SB_DH ** -0.5))),
        (OFF_SK, SB_W, identity),
        (OFF_SV, SB_W, identity),
        (OFF_GR, D_MODEL, gate),
        (OFF_GS, D_MODEL, gate),
    ]
    for r in range(0, o_ref.shape[0], PROJ_ROWS):
        rows = slice(r, r + PROJ_ROWS)
        hn = _rmsnorm_rows(h_ref[rows, :], g_ref[...]).astype(BF16)
        for first, width, epilogue in groups:
            for c in range(first, first + width, PROJ_TN):
                cols = slice(c, c + PROJ_TN)
                epilogue(jnp.dot(hn, w_ref[:, cols], preferred_element_type=F32), rows, cols)


def _inproj(h, g, w_in, layer, cos, sin):
    B, S, D = h.shape
    T = B * S
    tm = min(PROJ_TM, S)
    tiles_per_row = S // tm
    out = pl.pallas_call(
        _inproj_kernel,
        out_shape=jax.ShapeDtypeStruct((T, IN_COLS), BF16),
        grid=(T // tm,),
        in_specs=[
            pl.BlockSpec((tm, D), lambda i: (i, 0)),
            pl.BlockSpec((1, D), lambda i: (0, 0)),
            pl.BlockSpec((None, D, IN_COLS), lambda i: (layer, 0, 0), pipeline_mode=pl.Buffered(1)),
            pl.BlockSpec((tm, RET_DK), lambda i: (i % tiles_per_row, 0)),
            pl.BlockSpec((tm, RET_DK), lambda i: (i % tiles_per_row, 0)),
        ],
        out_specs=pl.BlockSpec((tm, IN_COLS), lambda i: (i, 0)),
        compiler_params=pltpu.CompilerParams(
            dimension_semantics=("parallel",), vmem_limit_bytes=VMEM_LIMIT),
        name="inproj",
    )(h.reshape(T, D), g, w_in, cos, sin)
    return out.reshape(B, S, IN_COLS)


def _retention_kernel(lg_ref, q_ref, k_ref, v_ref, g_ref, o_ref, kv_ref, r_ref):
    C = RET_CHUNK
    n_chunks = q_ref.shape[0] // C
    lg_row = lg_ref[...]
    lg = lg_row[:, :1]
    ii = lax.broadcasted_iota(jnp.int32, (C, C), 0)
    jj = lax.broadcasted_iota(jnp.int32, (C, C), 1)
    diff = (ii - jj).astype(F32)
    dec = jnp.where(diff >= 0, jnp.exp(lg_row * jnp.maximum(diff, 0.0)), 0.0)
    i_col = lax.broadcasted_iota(jnp.int32, (C, 1), 0).astype(F32)
    k_dec = jnp.exp(lg * (C - 1.0 - i_col))
    q_dec = jnp.exp(lg * (i_col + 1.0))
    chunk_dec = jnp.exp(lg * C)

    def chunk_rows(n):
        return pl.ds(pl.multiple_of(n * C, C), C)

    def kv_body(n, carry):
        rows = chunk_rows(n)
        kd = (k_ref[rows, :].astype(F32) * k_dec).astype(BF16)
        kv_ref[n] = lax.dot_general(kd, v_ref[rows, :], _TN, preferred_element_type=F32)
        return carry

    lax.fori_loop(0, n_chunks, kv_body, 0, unroll=RET_UNROLL)

    def scan_body(n, r):
        r_ref[n] = r.astype(BF16)
        return chunk_dec * r + kv_ref[n]

    lax.fori_loop(0, n_chunks, scan_body, jnp.zeros((RET_DK, RET_DV), F32), unroll=RET_UNROLL)

    def out_body(n, carry):
        rows = chunk_rows(n)
        q = q_ref[rows, :]
        v = v_ref[rows, :]
        s = lax.dot_general(q, k_ref[rows, :], _NT, preferred_element_type=F32) * dec
        lhs = jnp.concatenate([s.astype(BF16), (q.astype(F32) * q_dec).astype(BF16)], axis=1)
        rhs = jnp.concatenate([v, r_ref[n]], axis=0)
        o = jnp.dot(lhs, rhs, preferred_element_type=F32)
        mu = jnp.mean(o, axis=-1, keepdims=True)
        yc = o - mu
        var = jnp.mean(yc * yc, axis=-1, keepdims=True)
        y = yc * lax.rsqrt(var + NORM_EPS)
        o_ref[rows, :] = (y * g_ref[rows, :].astype(F32)).astype(BF16)
        return carry

    lax.fori_loop(0, n_chunks, out_body, 0, unroll=RET_UNROLL)


def _retention(proj, log_g):
    B, S, _ = proj.shape
    qb, kb = OFF_RQ // RET_DK, OFF_RK // RET_DK
    vb, gb = OFF_RV // RET_DV, OFF_RG // RET_DV
    return pl.pallas_call(
        _retention_kernel,
        out_shape=jax.ShapeDtypeStruct((B, S, RET_V), BF16),
        grid=(B, RET_HEADS),
        in_specs=[
            pl.BlockSpec((None, 1, LANES), lambda b, h: (h, 0, 0)),
            pl.BlockSpec((None, S, RET_DK), lambda b, h: (b, 0, qb + h)),
            pl.BlockSpec((None, S, RET_DK), lambda b, h: (b, 0, kb + h)),
            pl.BlockSpec((None, S, RET_DV), lambda b, h: (b, 0, vb + h)),
            pl.BlockSpec((None, S, RET_DV), lambda b, h: (b, 0, gb + h)),
        ],
        out_specs=pl.BlockSpec((None, S, RET_DV), lambda b, h: (b, 0, h)),
        scratch_shapes=[pltpu.VMEM((S // RET_CHUNK, RET_DK, RET_DV), F32),
                        pltpu.VMEM((S // RET_CHUNK, RET_DK, RET_DV), BF16)],
        compiler_params=pltpu.CompilerParams(
            dimension_semantics=("parallel", "parallel"), vmem_limit_bytes=VMEM_LIMIT),
        name="retention",
    )(log_g, proj, proj, proj, proj)


SB_PAIRS = SB_W // LANES
SB_STAGES = 5
SB_Z_SLOTS = 4
SB_UNROLL = 32
SB_DIAG_GROUP = 8
SOFTPLUS_CAP = 80.0


def _sb_schedule(n_blocks):
    tiles = [(qi, qi - s) for qi in range(1, n_blocks) for s in range(1, qi + 1)]
    pad = SB_STAGES - 1
    n_steps = -(-(len(tiles) + pad) // SB_UNROLL) * SB_UNROLL
    tab = np.zeros((3, n_steps + pad), np.int32)
    tab[1, :] = n_blocks
    for t, (qi, j) in enumerate(tiles):
        tab[:, t + pad] = (qi, qi, j)
    return tab


def _sb_kernel(tab_ref, q_ref, k_ref, v_ref, t_ref, o_ref,
               q2_ref, v2_ref, carry_ref, acc_ref, z_ref, sp_ref, cs_ref, a_ref):
    BLK = SB_BLOCK
    n_blocks = q_ref.shape[0] // BLK
    pad = SB_STAGES - 1
    n_steps = tab_ref.shape[1] - pad
    pairs = range(SB_PAIRS)
    cols = [slice(p * LANES, (p + 1) * LANES) for p in pairs]
    lane = lax.broadcasted_iota(jnp.int32, (BLK, LANES), 1)
    head0 = lane < SB_DH
    qi_loc = lax.broadcasted_iota(jnp.int32, (2 * BLK, BLK), 0) % BLK
    ki_loc = lax.broadcasted_iota(jnp.int32, (2 * BLK, BLK), 1)
    diag_mask = ki_loc < qi_loc

    def block_rows(j):
        return pl.ds(pl.multiple_of(j * BLK, BLK), BLK)

    def split_heads(x):
        zero = jnp.zeros_like(x)
        return jnp.concatenate([jnp.where(head0, x, zero), jnp.where(head0, zero, x)], axis=0)

    def softplus(z):
        return jnp.maximum(z, jnp.log(1.0 + jnp.exp(jnp.minimum(z, SOFTPLUS_CAP))))

    def stack_heads(a):
        ab = a.astype(BF16)
        return jnp.concatenate([ab[:BLK], ab[BLK:]], axis=1)

    def lane_bcast_col0(x):
        return jnp.broadcast_to(x[:, :1], x.shape)

    def prep(b, _):
        rows = block_rows(b)
        for p in pairs:
            q2_ref[b, p] = split_heads(q_ref[rows, cols[p]])
            v2_ref[b, p] = split_heads(v_ref[rows, cols[p]])
        return 0

    lax.fori_loop(0, n_blocks, prep, 0)
    z_ref[...] = jnp.zeros_like(z_ref)
    sp_ref[...] = jnp.zeros_like(sp_ref)
    cs_ref[...] = jnp.zeros_like(cs_ref)
    a_ref[...] = jnp.zeros_like(a_ref)
    carry_ref[n_blocks] = jnp.zeros(carry_ref.shape[1:], F32)
    acc_ref[n_blocks] = jnp.zeros(acc_ref.shape[1:], F32)

    def diag(i, _):
        work = [(SB_DIAG_GROUP * i + d, p) for d in range(SB_DIAG_GROUP) for p in pairs]
        zs = [lax.dot_general(q2_ref[qi, p], k_ref[block_rows(qi), cols[p]], _NT,
                              preferred_element_type=F32) for qi, p in work]
        sps = [jnp.where(diag_mask, softplus(z), 0.0).astype(BF16) for z in zs]
        css = [jnp.dot(sp, t_ref[...], preferred_element_type=F32) for sp in sps]
        a2s = []
        for (qi, p), z, suffix in zip(work, zs, css):
            a2s.append(stack_heads(jnp.where(diag_mask, jnp.exp(z - suffix), 0.0)))
            carry_ref[qi, p] = lane_bcast_col0(suffix)
        for (qi, p), a2 in zip(work, a2s):
            acc_ref[qi, p] = jnp.dot(a2, v2_ref[qi, p], preferred_element_type=F32)
        return 0

    lax.fori_loop(0, n_blocks // SB_DIAG_GROUP, diag, 0)

    def step(s, u):
        qi_qk, j_qk = tab_ref[0, s + pad], tab_ref[2, s + pad]
        slot_exp = tab_ref[1, s + pad - 3]
        slot_av, j_av = tab_ref[1, s], tab_ref[2, s]
        zslot_qk = u % SB_Z_SLOTS
        zslot_sp = (u - 1) % SB_Z_SLOTS
        zslot_exp = (u - 3) % SB_Z_SLOTS
        rows_qk = block_rows(j_qk)
        pvs = [jnp.dot(a_ref[p], v2_ref[j_av, p], preferred_element_type=F32) for p in pairs]
        css = [jnp.dot(sp_ref[p], t_ref[...], preferred_element_type=F32) for p in pairs]
        zs = [lax.dot_general(q2_ref[qi_qk, p], k_ref[rows_qk, cols[p]], _NT,
                              preferred_element_type=F32) for p in pairs]
        sps = [softplus(z_ref[zslot_sp, p]).astype(BF16) for p in pairs]
        a2s, carries = [], []
        for p in pairs:
            suffix = cs_ref[p] + carry_ref[slot_exp, p]
            a2s.append(stack_heads(jnp.exp(z_ref[zslot_exp, p] - suffix)))
            carries.append(lane_bcast_col0(suffix))
        for p in pairs:
            sp_ref[p] = sps[p]
            a_ref[p] = a2s[p]
            carry_ref[slot_exp, p] = carries[p]
            cs_ref[p] = css[p]
            z_ref[zslot_qk, p] = zs[p]
            acc_ref[slot_av, p] += pvs[p]

    def steps(i, _):
        for u in range(SB_UNROLL):
            step(i * SB_UNROLL + u, u)
        return 0

    lax.fori_loop(0, n_steps // SB_UNROLL, steps, 0)

    def emit(b, _):
        rows = block_rows(b)
        for p in pairs:
            o_ref[rows, cols[p]] = acc_ref[b, p].astype(BF16)
        return 0

    lax.fori_loop(0, n_blocks, emit, 0)


def _stick_breaking(proj, tri):
    B, S, _ = proj.shape
    n_blocks = S // SB_BLOCK
    assert n_blocks % SB_DIAG_GROUP == 0
    qb, kb, vb = OFF_SQ // SB_W, OFF_SK // SB_W, OFF_SV // SB_W
    tab = jnp.asarray(_sb_schedule(n_blocks))
    stacked = (SB_PAIRS, 2 * SB_BLOCK, SB_BLOCK)
    grid_spec = pltpu.PrefetchScalarGridSpec(
        num_scalar_prefetch=1,
        grid=(B,),
        in_specs=[
            pl.BlockSpec((None, S, SB_W), lambda b, tab: (b, 0, qb)),
            pl.BlockSpec((None, S, SB_W), lambda b, tab: (b, 0, kb)),
            pl.BlockSpec((None, S, SB_W), lambda b, tab: (b, 0, vb)),
            pl.BlockSpec((SB_BLOCK, SB_BLOCK), lambda b, tab: (0, 0)),
        ],
        out_specs=pl.BlockSpec((None, S, SB_W), lambda b, tab: (b, 0, 0)),
        scratch_shapes=[
            pltpu.VMEM((n_blocks,) + stacked, BF16),
            pltpu.VMEM((n_blocks,) + stacked, BF16),
            pltpu.VMEM((n_blocks + 1,) + stacked, F32),
            pltpu.VMEM((n_blocks + 1, SB_PAIRS, SB_BLOCK, LANES), F32),
            pltpu.VMEM((SB_Z_SLOTS,) + stacked, F32),
            pltpu.VMEM(stacked, BF16),
            pltpu.VMEM(stacked, F32),
            pltpu.VMEM((SB_PAIRS, SB_BLOCK, 2 * SB_BLOCK), BF16),
        ],
    )
    return pl.pallas_call(
        _sb_kernel,
        out_shape=jax.ShapeDtypeStruct((B, S, SB_W), BF16),
        grid_spec=grid_spec,
        compiler_params=pltpu.CompilerParams(
            dimension_semantics=("parallel",), vmem_limit_bytes=VMEM_LIMIT),
        name="stick_breaking",
    )(tab, proj, proj, proj, tri)


def _merge_kernel(h_ref, ret_ref, sb_ref, gr0_ref, gr1_ref, gs0_ref, gs1_ref,
                  wr_ref, ws_ref, wo_ref, o_ref):
    wr = wr_ref[...].astype(BF16)
    ws = ws_ref[...].astype(BF16)
    wo = wo_ref[...].astype(BF16)

    def branches(rows):
        ro = jnp.dot(ret_ref[rows, :], wr, preferred_element_type=F32)
        so = jnp.dot(sb_ref[rows, :], ws, preferred_element_type=F32)
        gr = jnp.concatenate([gr0_ref[rows, :], gr1_ref[rows, :]], axis=1).astype(F32)
        gs = jnp.concatenate([gs0_ref[rows, :], gs1_ref[rows, :]], axis=1).astype(F32)
        return (gr * ro + gs * so).astype(BF16)

    def project(rows, merged):
        o_ref[rows, :] = h_ref[rows, :] + jnp.dot(merged, wo, preferred_element_type=F32)

    chunks = [slice(r, r + MERGE_ROWS) for r in range(0, o_ref.shape[0], MERGE_ROWS)]
    prev = None
    for rows in chunks:
        merged = branches(rows)
        if prev is not None:
            project(*prev)
        prev = (rows, merged)
    project(*prev)


def _merge(h, ret, sb, proj, w_ret_o, w_sb_o, w_out, layer):
    B, S, D = h.shape
    tm = min(MERGE_TM, S)
    half = D_MODEL // 2
    gr0, gs0 = OFF_GR // half, OFF_GS // half
    tok = lambda width, col: pl.BlockSpec((None, tm, width), lambda b, i: (b, i, col))
    full = lambda a: pl.BlockSpec((None,) + a.shape[1:], lambda b, i: (layer, 0, 0),
                                  pipeline_mode=pl.Buffered(1))
    return pl.pallas_call(
        _merge_kernel,
        out_shape=jax.ShapeDtypeStruct((B, S, D), F32),
        grid=(B, S // tm),
        in_specs=[tok(D, 0), tok(RET_V, 0), tok(SB_W, 0),
                  tok(half, gr0), tok(half, gr0 + 1), tok(half, gs0), tok(half, gs0 + 1),
                  full(w_ret_o), full(w_sb_o), full(w_out)],
        out_specs=tok(D, 0),
        compiler_params=pltpu.CompilerParams(
            dimension_semantics=("parallel", "parallel"), vmem_limit_bytes=VMEM_LIMIT),
        name="merge",
    )(h, ret, sb, proj, proj, proj, proj, w_ret_o, w_sb_o, w_out)


def _ffn_kernel(h_ref, g_ref, wgu_ref, wd_ref, gf_ref, o_ref, *, final):
    def up(rows):
        hn = _rmsnorm_rows(h_ref[rows, :], g_ref[...]).astype(BF16)
        gu = jnp.dot(hn, wgu_ref[...], preferred_element_type=F32)
        g, u = gu[:, :FF], gu[:, FF:]
        return (g * _sigmoid(g) * u).astype(BF16)

    def down(rows, act):
        y = h_ref[rows, :] + jnp.dot(act, wd_ref[...], preferred_element_type=F32)
        if final:
            y = _rmsnorm_rows(y, gf_ref[...])
        o_ref[rows, :] = y

    chunks = [slice(r, r + FFN_ROWS) for r in range(0, o_ref.shape[0], FFN_ROWS)]
    prev = None
    for rows in chunks:
        act = up(rows)
        if prev is not None:
            down(*prev)
        prev = (rows, act)
    down(*prev)


def _ffn(h, g, w_gate_up, w_down, layer, g_final, final):
    B, S, D = h.shape
    T = B * S
    tm = min(FFN_TM, T)
    resident = lambda a: pl.BlockSpec((None,) + a.shape[1:], lambda i: (layer, 0, 0),
                                      pipeline_mode=pl.Buffered(1))
    out = pl.pallas_call(
        functools.partial(_ffn_kernel, final=final),
        out_shape=jax.ShapeDtypeStruct((T, D), F32),
        grid=(T // tm,),
        in_specs=[
            pl.BlockSpec((tm, D), lambda i: (i, 0)),
            pl.BlockSpec((1, D), lambda i: (0, 0)),
            resident(w_gate_up),
            resident(w_down),
            pl.BlockSpec((1, D), lambda i: (0, 0)),
        ],
        out_specs=pl.BlockSpec((tm, D), lambda i: (i, 0)),
        compiler_params=pltpu.CompilerParams(
            dimension_semantics=("parallel",), vmem_limit_bytes=VMEM_LIMIT),
        name="ffn_final" if final else "ffn",
    )(h.reshape(T, D), g, w_gate_up, w_down, g_final)
    return out.reshape(B, S, D)


def _rope_tables(S):
    half = RET_DK // 2
    pos = jnp.arange(S, dtype=F32)
    inv = 1.0 / (ROPE_BASE ** (jnp.arange(half, dtype=F32) / half))
    ang = pos[:, None] * inv[None, :]
    cos = jnp.cos(ang)
    sin = jnp.sin(ang)
    return jnp.concatenate([cos, cos], axis=-1), jnp.concatenate([-sin, sin], axis=-1)


def kernel(x, norm_mix, w_in, w_ret_o, w_sb_o, w_out, norm_ffn, w_gate_up, w_down, norm_final):
    B, S, D = x.shape
    depth = w_in.shape[0]
    cos, sin = _rope_tables(S)
    log_g = jnp.log1p(-jnp.exp2(-5.0 - jnp.arange(RET_HEADS, dtype=F32)))
    log_g = jnp.broadcast_to(log_g[:, None, None], (RET_HEADS, 1, LANES))
    r = jnp.arange(SB_BLOCK)[:, None]
    c = jnp.arange(SB_BLOCK)[None, :]
    tri = (r >= c).astype(BF16)
    g_final = norm_final.reshape(1, D)
    w_in_bf16 = w_in.astype(BF16)
    w_gate_up_bf16 = w_gate_up.astype(BF16)
    w_down_bf16 = w_down.astype(BF16)

    h = x
    for layer in range(depth):
        proj = _inproj(h, norm_mix[layer].reshape(1, D), w_in_bf16, layer, cos, sin)
        ret = _retention(proj, log_g)
        sb = _stick_breaking(proj, tri)
        h = _merge(h, ret, sb, proj, w_ret_o, w_sb_o, w_out, layer)
        h = _ffn(h, norm_ffn[layer].reshape(1, D), w_gate_up_bf16, w_down_bf16, layer, g_final,
                 final=(layer == depth - 1))
    return h
```

```python
import functools

import numpy as np
import jax
import jax.numpy as jnp
from jax import lax
from jax.experimental import pallas as pl
from jax.experimental.pallas import tpu as pltpu

F32 = jnp.float32
BF16 = jnp.bfloat16

D_MODEL = 1024
RET_HEADS = 4
RET_DK = 128
RET_DV = 256
RET_QK = RET_HEADS * RET_DK
RET_V = RET_HEADS * RET_DV
RET_CHUNK = 128
ROPE_BASE = 10000.0
SB_HEADS = 8
SB_DH = 64
SB_W = SB_HEADS * SB_DH
SB_BLOCK = 128
FF = 2816
NORM_EPS = 1e-6
IN_COLS = 2 * RET_QK + 2 * RET_V + 3 * SB_W + 2 * D_MODEL

OFF_RQ = 0
OFF_RK = OFF_RQ + RET_QK
OFF_RV = OFF_RK + RET_QK
OFF_RG = OFF_RV + RET_V
OFF_SQ = OFF_RG + RET_V
OFF_SK = OFF_SQ + SB_W
OFF_SV = OFF_SK + SB_W
OFF_GR = OFF_SV + SB_W
OFF_GS = OFF_GR + D_MODEL

LANES = 128
PROJ_TM = 512
PROJ_TN = 512
PROJ_ROWS = 256
FFN_TM = 1024
FFN_ROWS = 256
MERGE_TM = 1024
MERGE_ROWS = 256
RET_UNROLL = 16
VMEM_LIMIT = 56 * 1024 * 1024

_NT = (((1,), (1,)), ((), ()))
_TN = (((0,), (0,)), ((), ()))


def _sigmoid(x):
    return 1.0 / (1.0 + jnp.exp(-x))


def _rmsnorm_rows(x, g):
    ms = jnp.mean(x * x, axis=-1, keepdims=True)
    return x * lax.rsqrt(ms + NORM_EPS) * g


def _inproj_kernel(h_ref, g_ref, w_ref, cos_ref, sin_ref, o_ref):
    def rope(scale):
        def epilogue(y, rows, cols):
            cos = cos_ref[rows, :]
            sin = sin_ref[rows, :]
            for c in range(cols.start, cols.stop, RET_DK):
                yh = y[:, c - cols.start:c - cols.start + RET_DK]
                r = yh * cos + pltpu.roll(yh, RET_DK // 2, 1) * sin
                if scale != 1.0:
                    r = r * scale
                o_ref[rows, c:c + RET_DK] = r.astype(BF16)
        return epilogue

    def elementwise(fn):
        def epilogue(y, rows, cols):
            o_ref[rows, cols] = fn(y).astype(BF16)
        return epilogue

    identity = elementwise(lambda y: y)
    gate = elementwise(_sigmoid)
    groups = [
        (OFF_RQ, RET_QK, rope(1.0)),
        (OFF_RK, RET_QK, rope(RET_DK ** -0.5)),
        (OFF_RV, RET_V, identity),
        (OFF_RG, RET_V, elementwise(lambda y: y * _sigmoid(y))),
        (OFF_SQ, SB_W, elementwise(lambda y: y * (SB_DH ** -0.5))),
        (OFF_SK, SB_W, identity),
        (OFF_SV, SB_W, identity),
        (OFF_GR, D_MODEL, gate),
        (OFF_GS, D_MODEL, gate),
    ]
    for r in range(0, o_ref.shape[0], PROJ_ROWS):
        rows = slice(r, r + PROJ_ROWS)
        hn = _rmsnorm_rows(h_ref[rows, :], g_ref[...]).astype(BF16)
        for first, width, epilogue in groups:
            for c in range(first, first + width, PROJ_TN):
                cols = slice(c, c + PROJ_TN)
                epilogue(jnp.dot(hn, w_ref[:, cols], preferred_element_type=F32), rows, cols)


def _inproj(h, g, w_in, layer, cos, sin):
    B, S, D = h.shape
    T = B * S
    tm = min(PROJ_TM, S)
    tiles_per_row = S // tm
    out = pl.pallas_call(
        _inproj_kernel,
        out_shape=jax.ShapeDtypeStruct((T, IN_COLS), BF16),
        grid=(T // tm,),
        in_specs=[
            pl.BlockSpec((tm, D), lambda i: (i, 0)),
            pl.BlockSpec((1, D), lambda i: (0, 0)),
            pl.BlockSpec((None, D, IN_COLS), lambda i: (layer, 0, 0), pipeline_mode=pl.Buffered(1)),
            pl.BlockSpec((tm, RET_DK), lambda i: (i % tiles_per_row, 0)),
            pl.BlockSpec((tm, RET_DK), lambda i: (i % tiles_per_row, 0)),
        ],
        out_specs=pl.BlockSpec((tm, IN_COLS), lambda i: (i, 0)),
        compiler_params=pltpu.CompilerParams(
            dimension_semantics=("parallel",), vmem_limit_bytes=VMEM_LIMIT),
        name="inproj",
    )(h.reshape(T, D), g, w_in, cos, sin)
    return out.reshape(B, S, IN_COLS)


def _retention_kernel(lg_ref, q_ref, k_ref, v_ref, g_ref, o_ref, kv_ref, r_ref):
    C = RET_CHUNK
    n_chunks = q_ref.shape[0] // C
    ii = lax.broadcasted_iota(jnp.int32, (C, C), 0)
    jj = lax.broadcasted_iota(jnp.int32, (C, C), 1)
    diff = (ii - jj).astype(F32)
    i_col = lax.broadcasted_iota(jnp.int32, (C, 1), 0).astype(F32)

    def chunk_rows(n):
        return pl.ds(pl.multiple_of(n * C, C), C)

    for hd in range(RET_HEADS):
        qk_cols = slice(hd * RET_DK, (hd + 1) * RET_DK)
        v_cols = slice(hd * RET_DV, (hd + 1) * RET_DV)
        lg_row = lg_ref[hd]
        lg = lg_row[:, :1]
        dec = jnp.where(diff >= 0, jnp.exp(lg_row * jnp.maximum(diff, 0.0)), 0.0)
        k_dec = jnp.exp(lg * (C - 1.0 - i_col))
        q_dec = jnp.exp(lg * (i_col + 1.0))
        chunk_dec = jnp.exp(lg * C)

        def kv_body(n, carry):
            rows = chunk_rows(n)
            kd = (k_ref[rows, qk_cols].astype(F32) * k_dec).astype(BF16)
            kv_ref[hd, n] = lax.dot_general(kd, v_ref[rows, v_cols], _TN,
                                            preferred_element_type=F32)
            return carry

        lax.fori_loop(0, n_chunks, kv_body, 0, unroll=RET_UNROLL)

        def scan_body(n, r):
            r_ref[hd, n] = r.astype(BF16)
            return chunk_dec * r + kv_ref[hd, n]

        lax.fori_loop(0, n_chunks, scan_body, jnp.zeros((RET_DK, RET_DV), F32),
                      unroll=RET_UNROLL)

        def out_body(n, carry):
            rows = chunk_rows(n)
            q = q_ref[rows, qk_cols]
            v = v_ref[rows, v_cols]
            s = lax.dot_general(q, k_ref[rows, qk_cols], _NT, preferred_element_type=F32) * dec
            lhs = jnp.concatenate([s.astype(BF16), (q.astype(F32) * q_dec).astype(BF16)], axis=1)
            rhs = jnp.concatenate([v, r_ref[hd, n]], axis=0)
            o = jnp.dot(lhs, rhs, preferred_element_type=F32)
            mu = jnp.mean(o, axis=-1, keepdims=True)
            yc = o - mu
            var = jnp.mean(yc * yc, axis=-1, keepdims=True)
            y = yc * lax.rsqrt(var + NORM_EPS)
            o_ref[rows, v_cols] = (y * g_ref[rows, v_cols].astype(F32)).astype(BF16)
            return carry

        lax.fori_loop(0, n_chunks, out_body, 0, unroll=RET_UNROLL)


def _retention(proj, log_g):
    B, S, _ = proj.shape
    n_chunks = S // RET_CHUNK
    state = (RET_HEADS, n_chunks, RET_DK, RET_DV)
    return pl.pallas_call(
        _retention_kernel,
        out_shape=jax.ShapeDtypeStruct((B, S, RET_V), BF16),
        grid=(B,),
        in_specs=[
            pl.BlockSpec((RET_HEADS, 1, LANES), lambda b: (0, 0, 0)),
            pl.BlockSpec((None, S, RET_QK), lambda b: (b, 0, OFF_RQ // RET_QK)),
            pl.BlockSpec((None, S, RET_QK), lambda b: (b, 0, OFF_RK // RET_QK)),
            pl.BlockSpec((None, S, RET_V), lambda b: (b, 0, OFF_RV // RET_V)),
            pl.BlockSpec((None, S, RET_V), lambda b: (b, 0, OFF_RG // RET_V)),
        ],
        out_specs=pl.BlockSpec((None, S, RET_V), lambda b: (b, 0, 0)),
        scratch_shapes=[pltpu.VMEM(state, F32), pltpu.VMEM(state, BF16)],
        compiler_params=pltpu.CompilerParams(
            dimension_semantics=("parallel",), vmem_limit_bytes=VMEM_LIMIT),
        name="retention",
    )(log_g, proj, proj, proj, proj)


SB_PAIRS = SB_W // LANES
SB_STAGES = 5
SB_Z_SLOTS = 4
SB_UNROLL = 32
SB_DIAG_GROUP = 8
SOFTPLUS_CAP = 80.0


def _sb_schedule(n_blocks):
    tiles = [(qi, qi - s) for qi in range(1, n_blocks) for s in range(1, qi + 1)]
    pad = SB_STAGES - 1
    n_steps = -(-(len(tiles) + pad) // SB_UNROLL) * SB_UNROLL
    tab = np.zeros((3, n_steps + pad), np.int32)
    tab[1, :] = n_blocks
    for t, (qi, j) in enumerate(tiles):
        tab[:, t + pad] = (qi, qi, j)
    return tab


def _sb_kernel(tab_ref, q_ref, k_ref, v_ref, t_ref, o_ref,
               q2_ref, v2_ref, carry_ref, acc_ref, z_ref, sp_ref, cs_ref, a_ref):
    BLK = SB_BLOCK
    n_blocks = q_ref.shape[0] // BLK
    pad = SB_STAGES - 1
    n_steps = tab_ref.shape[1] - pad
    pairs = range(SB_PAIRS)
    cols = [slice(p * LANES, (p + 1) * LANES) for p in pairs]
    lane = lax.broadcasted_iota(jnp.int32, (BLK, LANES), 1)
    head0 = lane < SB_DH
    qi_loc = lax.broadcasted_iota(jnp.int32, (2 * BLK, BLK), 0) % BLK
    ki_loc = lax.broadcasted_iota(jnp.int32, (2 * BLK, BLK), 1)
    diag_mask = ki_loc < qi_loc

    def block_rows(j):
        return pl.ds(pl.multiple_of(j * BLK, BLK), BLK)

    def split_heads(x):
        zero = jnp.zeros_like(x)
        return jnp.concatenate([jnp.where(head0, x, zero), jnp.where(head0, zero, x)], axis=0)

    def softplus(z):
        return jnp.maximum(z, jnp.log(1.0 + jnp.exp(jnp.minimum(z, SOFTPLUS_CAP))))

    def stack_heads(a):
        ab = a.astype(BF16)
        return jnp.concatenate([ab[:BLK], ab[BLK:]], axis=1)

    def lane_bcast_col0(x):
        return jnp.broadcast_to(x[:, :1], x.shape)

    def prep(b, _):
        rows = block_rows(b)
        for p in pairs:
            q2_ref[b, p] = split_heads(q_ref[rows, cols[p]])
            v2_ref[b, p] = split_heads(v_ref[rows, cols[p]])
        return 0

    lax.fori_loop(0, n_blocks, prep, 0)
    z_ref[...] = jnp.zeros_like(z_ref)
    sp_ref[...] = jnp.zeros_like(sp_ref)
    cs_ref[...] = jnp.zeros_like(cs_ref)
    a_ref[...] = jnp.zeros_like(a_ref)
    carry_ref[n_blocks] = jnp.zeros(carry_ref.shape[1:], F32)
    acc_ref[n_blocks] = jnp.zeros(acc_ref.shape[1:], F32)

    def diag(i, _):
        work = [(SB_DIAG_GROUP * i + d, p) for d in range(SB_DIAG_GROUP) for p in pairs]
        zs = [lax.dot_general(q2_ref[qi, p], k_ref[block_rows(qi), cols[p]], _NT,
                              preferred_element_type=F32) for qi, p in work]
        sps = [jnp.where(diag_mask, softplus(z), 0.0).astype(BF16) for z in zs]
        css = [jnp.dot(sp, t_ref[...], preferred_element_type=F32) for sp in sps]
        a2s = []
        for (qi, p), z, suffix in zip(work, zs, css):
            a2s.append(stack_heads(jnp.where(diag_mask, jnp.exp(z - suffix), 0.0)))
            carry_ref[qi, p] = lane_bcast_col0(suffix)
        for (qi, p), a2 in zip(work, a2s):
            acc_ref[qi, p] = jnp.dot(a2, v2_ref[qi, p], preferred_element_type=F32)
        return 0

    lax.fori_loop(0, n_blocks // SB_DIAG_GROUP, diag, 0)

    def step(s, u):
        qi_qk, j_qk = tab_ref[0, s + pad], tab_ref[2, s + pad]
        slot_exp = tab_ref[1, s + pad - 3]
        slot_av, j_av = tab_ref[1, s], tab_ref[2, s]
        zslot_qk = u % SB_Z_SLOTS
        zslot_sp = (u - 1) % SB_Z_SLOTS
        zslot_exp = (u - 3) % SB_Z_SLOTS
        rows_qk = block_rows(j_qk)
        pvs = [jnp.dot(a_ref[p], v2_ref[j_av, p], preferred_element_type=F32) for p in pairs]
        css = [jnp.dot(sp_ref[p], t_ref[...], preferred_element_type=F32) for p in pairs]
        zs = [lax.dot_general(q2_ref[qi_qk, p], k_ref[rows_qk, cols[p]], _NT,
                              preferred_element_type=F32) for p in pairs]
        sps = [softplus(z_ref[zslot_sp, p]).astype(BF16) for p in pairs]
        a2s, carries = [], []
        for p in pairs:
            suffix = cs_ref[p] + carry_ref[slot_exp, p]
            a2s.append(stack_heads(jnp.exp(z_ref[zslot_exp, p] - suffix)))
            carries.append(lane_bcast_col0(suffix))
        for p in pairs:
            sp_ref[p] = sps[p]
            a_ref[p] = a2s[p]
            carry_ref[slot_exp, p] = carries[p]
            cs_ref[p] = css[p]
            z_ref[zslot_qk, p] = zs[p]
            acc_ref[slot_av, p] += pvs[p]

    def steps(i, _):
        for u in range(SB_UNROLL):
            step(i * SB_UNROLL + u, u)
        return 0

    lax.fori_loop(0, n_steps // SB_UNROLL, steps, 0)

    def emit(b, _):
        rows = block_rows(b)
        for p in pairs:
            o_ref[rows, cols[p]] = acc_ref[b, p].astype(BF16)
        return 0

    lax.fori_loop(0, n_blocks, emit, 0)


def _stick_breaking(proj, tri):
    B, S, _ = proj.shape
    n_blocks = S // SB_BLOCK
    assert n_blocks % SB_DIAG_GROUP == 0
    qb, kb, vb = OFF_SQ // SB_W, OFF_SK // SB_W, OFF_SV // SB_W
    tab = jnp.asarray(_sb_schedule(n_blocks))
    stacked = (SB_PAIRS, 2 * SB_BLOCK, SB_BLOCK)
    grid_spec = pltpu.PrefetchScalarGridSpec(
        num_scalar_prefetch=1,
        grid=(B,),
        in_specs=[
            pl.BlockSpec((None, S, SB_W), lambda b, tab: (b, 0, qb)),
            pl.BlockSpec((None, S, SB_W), lambda b, tab: (b, 0, kb)),
            pl.BlockSpec((None, S, SB_W), lambda b, tab: (b, 0, vb)),
            pl.BlockSpec((SB_BLOCK, SB_BLOCK), lambda b, tab: (0, 0)),
        ],
        out_specs=pl.BlockSpec((None, S, SB_W), lambda b, tab: (b, 0, 0)),
        scratch_shapes=[
            pltpu.VMEM((n_blocks,) + stacked, BF16),
            pltpu.VMEM((n_blocks,) + stacked, BF16),
            pltpu.VMEM((n_blocks + 1,) + stacked, F32),
            pltpu.VMEM((n_blocks + 1, SB_PAIRS, SB_BLOCK, LANES), F32),
            pltpu.VMEM((SB_Z_SLOTS,) + stacked, F32),
            pltpu.VMEM(stacked, BF16),
            pltpu.VMEM(stacked, F32),
            pltpu.VMEM((SB_PAIRS, SB_BLOCK, 2 * SB_BLOCK), BF16),
        ],
    )
    return pl.pallas_call(
        _sb_kernel,
        out_shape=jax.ShapeDtypeStruct((B, S, SB_W), BF16),
        grid_spec=grid_spec,
        compiler_params=pltpu.CompilerParams(
            dimension_semantics=("parallel",), vmem_limit_bytes=VMEM_LIMIT),
        name="stick_breaking",
    )(tab, proj, proj, proj, tri)


def _merge_kernel(h_ref, ret_ref, sb_ref, gr0_ref, gr1_ref, gs0_ref, gs1_ref,
                  wr_ref, ws_ref, wo_ref, o_ref):
    wr = wr_ref[...].astype(BF16)
    ws = ws_ref[...].astype(BF16)
    wo = wo_ref[...].astype(BF16)

    def branches(rows):
        ro = jnp.dot(ret_ref[rows, :], wr, preferred_element_type=F32)
        so = jnp.dot(sb_ref[rows, :], ws, preferred_element_type=F32)
        gr = jnp.concatenate([gr0_ref[rows, :], gr1_ref[rows, :]], axis=1).astype(F32)
        gs = jnp.concatenate([gs0_ref[rows, :], gs1_ref[rows, :]], axis=1).astype(F32)
        return (gr * ro + gs * so).astype(BF16)

    def project(rows, merged):
        o_ref[rows, :] = h_ref[rows, :] + jnp.dot(merged, wo, preferred_element_type=F32)

    chunks = [slice(r, r + MERGE_ROWS) for r in range(0, o_ref.shape[0], MERGE_ROWS)]
    prev = None
    for rows in chunks:
        merged = branches(rows)
        if prev is not None:
            project(*prev)
        prev = (rows, merged)
    project(*prev)


def _merge(h, ret, sb, proj, w_ret_o, w_sb_o, w_out, layer):
    B, S, D = h.shape
    tm = min(MERGE_TM, S)
    half = D_MODEL // 2
    gr0, gs0 = OFF_GR // half, OFF_GS // half
    tok = lambda width, col: pl.BlockSpec((None, tm, width), lambda b, i: (b, i, col))
    full = lambda a: pl.BlockSpec((None,) + a.shape[1:], lambda b, i: (layer, 0, 0),
                                  pipeline_mode=pl.Buffered(1))
    return pl.pallas_call(
        _merge_kernel,
        out_shape=jax.ShapeDtypeStruct((B, S, D), F32),
        grid=(B, S // tm),
        in_specs=[tok(D, 0), tok(RET_V, 0), tok(SB_W, 0),
                  tok(half, gr0), tok(half, gr0 + 1), tok(half, gs0), tok(half, gs0 + 1),
                  full(w_ret_o), full(w_sb_o), full(w_out)],
        out_specs=tok(D, 0),
        compiler_params=pltpu.CompilerParams(
            dimension_semantics=("parallel", "parallel"), vmem_limit_bytes=VMEM_LIMIT),
        name="merge",
    )(h, ret, sb, proj, proj, proj, proj, w_ret_o, w_sb_o, w_out)


def _ffn_kernel(h_ref, g_ref, wgu_ref, wd_ref, gf_ref, o_ref, *, final):
    def up(rows):
        hn = _rmsnorm_rows(h_ref[rows, :], g_ref[...]).astype(BF16)
        gu = jnp.dot(hn, wgu_ref[...], preferred_element_type=F32)
        g, u = gu[:, :FF], gu[:, FF:]
        return (g * _sigmoid(g) * u).astype(BF16)

    def down(rows, act):
        y = h_ref[rows, :] + jnp.dot(act, wd_ref[...], preferred_element_type=F32)
        if final:
            y = _rmsnorm_rows(y, gf_ref[...])
        o_ref[rows, :] = y

    chunks = [slice(r, r + FFN_ROWS) for r in range(0, o_ref.shape[0], FFN_ROWS)]
    prev = None
    for rows in chunks:
        act = up(rows)
        if prev is not None:
            down(*prev)
        prev = (rows, act)
    down(*prev)


def _ffn(h, g, w_gate_up, w_down, layer, g_final, final):
    B, S, D = h.shape
    T = B * S
    tm = min(FFN_TM, T)
    resident = lambda a: pl.BlockSpec((None,) + a.shape[1:], lambda i: (layer, 0, 0),
                                      pipeline_mode=pl.Buffered(1))
    out = pl.pallas_call(
        functools.partial(_ffn_kernel, final=final),
        out_shape=jax.ShapeDtypeStruct((T, D), F32),
        grid=(T // tm,),
        in_specs=[
            pl.BlockSpec((tm, D), lambda i: (i, 0)),
            pl.BlockSpec((1, D), lambda i: (0, 0)),
            resident(w_gate_up),
            resident(w_down),
            pl.BlockSpec((1, D), lambda i: (0, 0)),
        ],
        out_specs=pl.BlockSpec((tm, D), lambda i: (i, 0)),
        compiler_params=pltpu.CompilerParams(
            dimension_semantics=("parallel",), vmem_limit_bytes=VMEM_LIMIT),
        name="ffn_final" if final else "ffn",
    )(h.reshape(T, D), g, w_gate_up, w_down, g_final)
    return out.reshape(B, S, D)


def _rope_tables(S):
    half = RET_DK // 2
    pos = jnp.arange(S, dtype=F32)
    inv = 1.0 / (ROPE_BASE ** (jnp.arange(half, dtype=F32) / half))
    ang = pos[:, None] * inv[None, :]
    cos = jnp.cos(ang)
    sin = jnp.sin(ang)
    return jnp.concatenate([cos, cos], axis=-1), jnp.concatenate([-sin, sin], axis=-1)


def kernel(x, norm_mix, w_in, w_ret_o, w_sb_o, w_out, norm_ffn, w_gate_up, w_down, norm_final):
    B, S, D = x.shape
    depth = w_in.shape[0]
    cos, sin = _rope_tables(S)
    log_g = jnp.log1p(-jnp.exp2(-5.0 - jnp.arange(RET_HEADS, dtype=F32)))
    log_g = jnp.broadcast_to(log_g[:, None, None], (RET_HEADS, 1, LANES))
    r = jnp.arange(SB_BLOCK)[:, None]
    c = jnp.arange(SB_BLOCK)[None, :]
    tri = (r >= c).astype(BF16)
    g_final = norm_final.reshape(1, D)
    w_in_bf16 = w_in.astype(BF16)
    w_gate_up_bf16 = w_gate_up.astype(BF16)
    w_down_bf16 = w_down.astype(BF16)

    h = x
    for layer in range(depth):
        proj = _inproj(h, norm_mix[layer].reshape(1, D), w_in_bf16, layer, cos, sin)
        ret = _retention(proj, log_g)
        sb = _stick_breaking(proj, tri)
        h = _merge(h, ret, sb, proj, w_ret_o, w_sb_o, w_out, layer)
        h = _ffn(h, norm_ffn[layer].reshape(1, D), w_gate_up_bf16, w_down_bf16, layer, g_final,
                 final=(layer == depth - 1))
    return h
```

```python
import functools

import numpy as np
import jax
import jax.numpy as jnp
from jax import lax
from jax.experimental import pallas as pl
from jax.experimental.pallas import tpu as pltpu

F32 = jnp.float32
BF16 = jnp.bfloat16

D_MODEL = 1024
RET_HEADS = 4
RET_DK = 128
RET_DV = 256
RET_QK = RET_HEADS * RET_DK
RET_V = RET_HEADS * RET_DV
RET_CHUNK = 128
ROPE_BASE = 10000.0
SB_HEADS = 8
SB_DH = 64
SB_W = SB_HEADS * SB_DH
SB_BLOCK = 128
FF = 2816
NORM_EPS = 1e-6
IN_COLS = 2 * RET_QK + 2 * RET_V + 3 * SB_W + 2 * D_MODEL

OFF_RQ = 0
OFF_RK = OFF_RQ + RET_QK
OFF_RV = OFF_RK + RET_QK
OFF_RG = OFF_RV + RET_V
OFF_SQ = OFF_RG + RET_V
OFF_SK = OFF_SQ + SB_W
OFF_SV = OFF_SK + SB_W
OFF_GR = OFF_SV + SB_W
OFF_GS = OFF_GR + D_MODEL

LANES = 128
PROJ_TM = 512
PROJ_TN = 512
PROJ_ROWS = 256
FFN_TM = 1024
FFN_ROWS = 256
MERGE_TM = 1024
MERGE_ROWS = 256
RET_UNROLL = 16
VMEM_LIMIT = 56 * 1024 * 1024

_NT = (((1,), (1,)), ((), ()))
_TN = (((0,), (0,)), ((), ()))


def _sigmoid(x):
    return 1.0 / (1.0 + jnp.exp(-x))


def _rmsnorm_rows(x, g):
    ms = jnp.mean(x * x, axis=-1, keepdims=True)
    return x * lax.rsqrt(ms + NORM_EPS) * g


def _inproj_kernel(h_ref, g_ref, w_ref, cos_ref, sin_ref, o_ref):
    def rope(scale):
        def epilogue(y, rows, cols):
            cos = cos_ref[rows, :]
            sin = sin_ref[rows, :]
            for c in range(cols.start, cols.stop, RET_DK):
                yh = y[:, c - cols.start:c - cols.start + RET_DK]
                r = yh * cos + pltpu.roll(yh, RET_DK // 2, 1) * sin
                if scale != 1.0:
                    r = r * scale
                o_ref[rows, c:c + RET_DK] = r.astype(BF16)
        return epilogue

    def elementwise(fn):
        def epilogue(y, rows, cols):
            o_ref[rows, cols] = fn(y).astype(BF16)
        return epilogue

    identity = elementwise(lambda y: y)
    gate = elementwise(_sigmoid)
    groups = [
        (OFF_RQ, RET_QK, rope(1.0)),
        (OFF_RK, RET_QK, rope(RET_DK ** -0.5)),
        (OFF_RV, RET_V, identity),
        (OFF_RG, RET_V, elementwise(lambda y: y * _sigmoid(y))),
        (OFF_SQ, SB_W, elementwise(lambda y: y * (SB_DH ** -0.5))),
        (OFF_SK, SB_W, identity),
        (OFF_SV, SB_W, identity),
        (OFF_GR, D_MODEL, gate),
        (OFF_GS, D_MODEL, gate),
    ]
    for r in range(0, o_ref.shape[0], PROJ_ROWS):
        rows = slice(r, r + PROJ_ROWS)
        hn = _rmsnorm_rows(h_ref[rows, :], g_ref[...]).astype(BF16)
        for first, width, epilogue in groups:
            for c in range(first, first + width, PROJ_TN):
                cols = slice(c, c + PROJ_TN)
                epilogue(jnp.dot(hn, w_ref[:, cols], preferred_element_type=F32), rows, cols)


def _inproj(h, g, w_in, layer, cos, sin):
    B, S, D = h.shape
    T = B * S
    tm = min(PROJ_TM, S)
    tiles_per_row = S // tm
    out = pl.pallas_call(
        _inproj_kernel,
        out_shape=jax.ShapeDtypeStruct((T, IN_COLS), BF16),
        grid=(T // tm,),
        in_specs=[
            pl.BlockSpec((tm, D), lambda i: (i, 0)),
            pl.BlockSpec((1, D), lambda i: (0, 0)),
            pl.BlockSpec((None, D, IN_COLS), lambda i: (layer, 0, 0), pipeline_mode=pl.Buffered(1)),
            pl.BlockSpec((tm, RET_DK), lambda i: (i % tiles_per_row, 0)),
            pl.BlockSpec((tm, RET_DK), lambda i: (i % tiles_per_row, 0)),
        ],
        out_specs=pl.BlockSpec((tm, IN_COLS), lambda i: (i, 0)),
        compiler_params=pltpu.CompilerParams(
            dimension_semantics=("parallel",), vmem_limit_bytes=VMEM_LIMIT),
        name="inproj",
    )(h.reshape(T, D), g, w_in, cos, sin)
    return out.reshape(B, S, IN_COLS)


def _retention_kernel(lg_ref, q_ref, k_ref, v_ref, g_ref, o_ref, kv_ref, r_ref):
    C = RET_CHUNK
    n_chunks = q_ref.shape[0] // C
    ii = lax.broadcasted_iota(jnp.int32, (C, C), 0)
    jj = lax.broadcasted_iota(jnp.int32, (C, C), 1)
    diff = (ii - jj).astype(F32)
    i_col = lax.broadcasted_iota(jnp.int32, (C, 1), 0).astype(F32)

    def chunk_rows(n):
        return pl.ds(pl.multiple_of(n * C, C), C)

    for hd in range(RET_HEADS):
        qk_cols = slice(hd * RET_DK, (hd + 1) * RET_DK)
        v_cols = slice(hd * RET_DV, (hd + 1) * RET_DV)
        lg_row = lg_ref[hd]
        lg = lg_row[:, :1]
        dec = jnp.where(diff >= 0, jnp.exp(lg_row * jnp.maximum(diff, 0.0)), 0.0)
        k_dec = jnp.exp(lg * (C - 1.0 - i_col))
        q_dec = jnp.exp(lg * (i_col + 1.0))
        chunk_dec = jnp.exp(lg * C)

        def kv_body(n, carry):
            rows = chunk_rows(n)
            kd = (k_ref[rows, qk_cols].astype(F32) * k_dec).astype(BF16)
            kv_ref[hd, n] = lax.dot_general(kd, v_ref[rows, v_cols], _TN,
                                            preferred_element_type=F32)
            return carry

        lax.fori_loop(0, n_chunks, kv_body, 0, unroll=RET_UNROLL)

        def scan_body(n, r):
            r_ref[hd, n] = r.astype(BF16)
            return chunk_dec * r + kv_ref[hd, n]

        lax.fori_loop(0, n_chunks, scan_body, jnp.zeros((RET_DK, RET_DV), F32),
                      unroll=RET_UNROLL)

        def out_body(n, carry):
            rows = chunk_rows(n)
            q = q_ref[rows, qk_cols]
            v = v_ref[rows, v_cols]
            s = lax.dot_general(q, k_ref[rows, qk_cols], _NT, preferred_element_type=F32) * dec
            lhs = jnp.concatenate([s.astype(BF16), (q.astype(F32) * q_dec).astype(BF16)], axis=1)
            rhs = jnp.concatenate([v, r_ref[hd, n]], axis=0)
            o = jnp.dot(lhs, rhs, preferred_element_type=F32)
            mu = jnp.mean(o, axis=-1, keepdims=True)
            yc = o - mu
            var = jnp.mean(yc * yc, axis=-1, keepdims=True)
            y = yc * lax.rsqrt(var + NORM_EPS)
            o_ref[rows, v_cols] = (y * g_ref[rows, v_cols].astype(F32)).astype(BF16)
            return carry

        lax.fori_loop(0, n_chunks, out_body, 0, unroll=RET_UNROLL)


def _retention(proj, log_g):
    B, S, _ = proj.shape
    n_chunks = S // RET_CHUNK
    state = (RET_HEADS, n_chunks, RET_DK, RET_DV)
    return pl.pallas_call(
        _retention_kernel,
        out_shape=jax.ShapeDtypeStruct((B, S, RET_V), BF16),
        grid=(B,),
        in_specs=[
            pl.BlockSpec((RET_HEADS, 1, LANES), lambda b: (0, 0, 0)),
            pl.BlockSpec((None, S, RET_QK), lambda b: (b, 0, OFF_RQ // RET_QK)),
            pl.BlockSpec((None, S, RET_QK), lambda b: (b, 0, OFF_RK // RET_QK)),
            pl.BlockSpec((None, S, RET_V), lambda b: (b, 0, OFF_RV // RET_V)),
            pl.BlockSpec((None, S, RET_V), lambda b: (b, 0, OFF_RG // RET_V)),
        ],
        out_specs=pl.BlockSpec((None, S, RET_V), lambda b: (b, 0, 0)),
        scratch_shapes=[pltpu.VMEM(state, F32), pltpu.VMEM(state, BF16)],
        compiler_params=pltpu.CompilerParams(
            dimension_semantics=("parallel",), vmem_limit_bytes=VMEM_LIMIT),
        name="retention",
    )(log_g, proj, proj, proj, proj)


SB_PAIRS = SB_W // LANES
SB_STAGES = 5
SB_Z_SLOTS = 4
SB_UNROLL = 32
SB_DIAG_GROUP = 8
SOFTPLUS_CAP = 80.0


def _sb_schedule(n_blocks):
    tiles = [(qi, qi - s) for qi in range(1, n_blocks) for s in range(1, qi + 1)]
    pad = SB_STAGES - 1
    n_steps = -(-(len(tiles) + pad) // SB_UNROLL) * SB_UNROLL
    tab = np.zeros((3, n_steps + pad), np.int32)
    tab[1, :] = n_blocks
    for t, (qi, j) in enumerate(tiles):
        tab[:, t + pad] = (qi, qi, j)
    return tab


def _sb_kernel(tab_ref, q_ref, k_ref, v_ref, t_ref, o_ref,
               q2_ref, v2_ref, carry_ref, acc_ref, z_ref, sp_ref, cs_ref, a_ref):
    BLK = SB_BLOCK
    n_blocks = q_ref.shape[0] // BLK
    pad = SB_STAGES - 1
    n_steps = tab_ref.shape[1] - pad
    pairs = range(SB_PAIRS)
    cols = [slice(p * LANES, (p + 1) * LANES) for p in pairs]
    lane = lax.broadcasted_iota(jnp.int32, (BLK, LANES), 1)
    head0 = lane < SB_DH
    qi_loc = lax.broadcasted_iota(jnp.int32, (2 * BLK, BLK), 0) % BLK
    ki_loc = lax.broadcasted_iota(jnp.int32, (2 * BLK, BLK), 1)
    diag_mask = ki_loc < qi_loc

    def block_rows(j):
        return pl.ds(pl.multiple_of(j * BLK, BLK), BLK)

    def split_heads(x):
        zero = jnp.zeros_like(x)
        return jnp.concatenate([jnp.where(head0, x, zero), jnp.where(head0, zero, x)], axis=0)

    def softplus(z):
        return jnp.maximum(z, jnp.log(1.0 + jnp.exp(jnp.minimum(z, SOFTPLUS_CAP))))

    def stack_heads(a):
        ab = a.astype(BF16)
        return jnp.concatenate([ab[:BLK], ab[BLK:]], axis=1)

    def lane_bcast_col0(x):
        return jnp.broadcast_to(x[:, :1], x.shape)

    def prep(b, _):
        rows = block_rows(b)
        for p in pairs:
            q2_ref[b, p] = split_heads(q_ref[rows, cols[p]])
            v2_ref[b, p] = split_heads(v_ref[rows, cols[p]])
        return 0

    lax.fori_loop(0, n_blocks, prep, 0)
    z_ref[...] = jnp.zeros_like(z_ref)
    sp_ref[...] = jnp.zeros_like(sp_ref)
    cs_ref[...] = jnp.zeros_like(cs_ref)
    a_ref[...] = jnp.zeros_like(a_ref)
    carry_ref[n_blocks] = jnp.zeros(carry_ref.shape[1:], F32)
    acc_ref[n_blocks] = jnp.zeros(acc_ref.shape[1:], F32)

    def diag(i, _):
        work = [(SB_DIAG_GROUP * i + d, p) for d in range(SB_DIAG_GROUP) for p in pairs]
        zs = [lax.dot_general(q2_ref[qi, p], k_ref[block_rows(qi), cols[p]], _NT,
                              preferred_element_type=F32) for qi, p in work]
        sps = [jnp.where(diag_mask, softplus(z), 0.0).astype(BF16) for z in zs]
        css = [jnp.dot(sp, t_ref[...], preferred_element_type=F32) for sp in sps]
        a2s = []
        for (qi, p), z, suffix in zip(work, zs, css):
            a2s.append(stack_heads(jnp.where(diag_mask, jnp.exp(z - suffix), 0.0)))
            carry_ref[qi, p] = lane_bcast_col0(suffix)
        for (qi, p), a2 in zip(work, a2s):
            acc_ref[qi, p] = jnp.dot(a2, v2_ref[qi, p], preferred_element_type=F32)
        return 0

    lax.fori_loop(0, n_blocks // SB_DIAG_GROUP, diag, 0)

    def step(s, u):
        qi_qk, j_qk = tab_ref[0, s + pad], tab_ref[2, s + pad]
        slot_exp = tab_ref[1, s + pad - 3]
        slot_av, j_av = tab_ref[1, s], tab_ref[2, s]
        zslot_qk = u % SB_Z_SLOTS
        zslot_sp = (u - 1) % SB_Z_SLOTS
        zslot_exp = (u - 3) % SB_Z_SLOTS
        rows_qk = block_rows(j_qk)
        pvs = [jnp.dot(a_ref[p, :, :BLK], v2_ref[j_av, p, :BLK], preferred_element_type=F32)
               + jnp.dot(a_ref[p, :, BLK:], v2_ref[j_av, p, BLK:], preferred_element_type=F32)
               for p in pairs]
        css = [jnp.dot(sp_ref[p], t_ref[...], preferred_element_type=F32) for p in pairs]
        zs = [lax.dot_general(q2_ref[qi_qk, p], k_ref[rows_qk, cols[p]], _NT,
                              preferred_element_type=F32) for p in pairs]
        sps = [softplus(z_ref[zslot_sp, p]).astype(BF16) for p in pairs]
        a2s, carries = [], []
        for p in pairs:
            suffix = cs_ref[p] + carry_ref[slot_exp, p]
            a2s.append(stack_heads(jnp.exp(z_ref[zslot_exp, p] - suffix)))
            carries.append(lane_bcast_col0(suffix))
        for p in pairs:
            sp_ref[p] = sps[p]
            a_ref[p] = a2s[p]
            carry_ref[slot_exp, p] = carries[p]
            cs_ref[p] = css[p]
            z_ref[zslot_qk, p] = zs[p]
            acc_ref[slot_av, p] += pvs[p]

    def steps(i, _):
        for u in range(SB_UNROLL):
            step(i * SB_UNROLL + u, u)
        return 0

    lax.fori_loop(0, n_steps // SB_UNROLL, steps, 0)

    def emit(b, _):
        rows = block_rows(b)
        for p in pairs:
            o_ref[rows, cols[p]] = acc_ref[b, p].astype(BF16)
        return 0

    lax.fori_loop(0, n_blocks, emit, 0)


def _stick_breaking(proj, tri):
    B, S, _ = proj.shape
    n_blocks = S // SB_BLOCK
    assert n_blocks % SB_DIAG_GROUP == 0
    qb, kb, vb = OFF_SQ // SB_W, OFF_SK // SB_W, OFF_SV // SB_W
    tab = jnp.asarray(_sb_schedule(n_blocks))
    stacked = (SB_PAIRS, 2 * SB_BLOCK, SB_BLOCK)
    grid_spec = pltpu.PrefetchScalarGridSpec(
        num_scalar_prefetch=1,
        grid=(B,),
        in_specs=[
            pl.BlockSpec((None, S, SB_W), lambda b, tab: (b, 0, qb)),
            pl.BlockSpec((None, S, SB_W), lambda b, tab: (b, 0, kb)),
            pl.BlockSpec((None, S, SB_W), lambda b, tab: (b, 0, vb)),
            pl.BlockSpec((SB_BLOCK, SB_BLOCK), lambda b, tab: (0, 0)),
        ],
        out_specs=pl.BlockSpec((None, S, SB_W), lambda b, tab: (b, 0, 0)),
        scratch_shapes=[
            pltpu.VMEM((n_blocks,) + stacked, BF16),
            pltpu.VMEM((n_blocks,) + stacked, BF16),
            pltpu.VMEM((n_blocks + 1,) + stacked, F32),
            pltpu.VMEM((n_blocks + 1, SB_PAIRS, SB_BLOCK, LANES), F32),
            pltpu.VMEM((SB_Z_SLOTS,) + stacked, F32),
            pltpu.VMEM(stacked, BF16),
            pltpu.VMEM(stacked, F32),
            pltpu.VMEM((SB_PAIRS, SB_BLOCK, 2 * SB_BLOCK), BF16),
        ],
    )
    return pl.pallas_call(
        _sb_kernel,
        out_shape=jax.ShapeDtypeStruct((B, S, SB_W), BF16),
        grid_spec=grid_spec,
        compiler_params=pltpu.CompilerParams(
            dimension_semantics=("parallel",), vmem_limit_bytes=VMEM_LIMIT),
        name="stick_breaking",
    )(tab, proj, proj, proj, tri)


def _merge_kernel(h_ref, ret_ref, sb_ref, gr0_ref, gr1_ref, gs0_ref, gs1_ref,
                  wr_ref, ws_ref, wo_ref, o_ref):
    wr = wr_ref[...].astype(BF16)
    ws = ws_ref[...].astype(BF16)
    wo = wo_ref[...].astype(BF16)

    def branches(rows):
        ro = jnp.dot(ret_ref[rows, :], wr, preferred_element_type=F32)
        so = jnp.dot(sb_ref[rows, :], ws, preferred_element_type=F32)
        gr = jnp.concatenate([gr0_ref[rows, :], gr1_ref[rows, :]], axis=1).astype(F32)
        gs = jnp.concatenate([gs0_ref[rows, :], gs1_ref[rows, :]], axis=1).astype(F32)
        return (gr * ro + gs * so).astype(BF16)

    def project(rows, merged):
        o_ref[rows, :] = h_ref[rows, :] + jnp.dot(merged, wo, preferred_element_type=F32)

    chunks = [slice(r, r + MERGE_ROWS) for r in range(0, o_ref.shape[0], MERGE_ROWS)]
    prev = None
    for rows in chunks:
        merged = branches(rows)
        if prev is not None:
            project(*prev)
        prev = (rows, merged)
    project(*prev)


def _merge(h, ret, sb, proj, w_ret_o, w_sb_o, w_out, layer):
    B, S, D = h.shape
    tm = min(MERGE_TM, S)
    half = D_MODEL // 2
    gr0, gs0 = OFF_GR // half, OFF_GS // half
    tok = lambda width, col: pl.BlockSpec((None, tm, width), lambda b, i: (b, i, col))
    full = lambda a: pl.BlockSpec((None,) + a.shape[1:], lambda b, i: (layer, 0, 0),
                                  pipeline_mode=pl.Buffered(1))
    return pl.pallas_call(
        _merge_kernel,
        out_shape=jax.ShapeDtypeStruct((B, S, D), F32),
        grid=(B, S // tm),
        in_specs=[tok(D, 0), tok(RET_V, 0), tok(SB_W, 0),
                  tok(half, gr0), tok(half, gr0 + 1), tok(half, gs0), tok(half, gs0 + 1),
                  full(w_ret_o), full(w_sb_o), full(w_out)],
        out_specs=tok(D, 0),
        compiler_params=pltpu.CompilerParams(
            dimension_semantics=("parallel", "parallel"), vmem_limit_bytes=VMEM_LIMIT),
        name="merge",
    )(h, ret, sb, proj, proj, proj, proj, w_ret_o, w_sb_o, w_out)


def _ffn_kernel(h_ref, g_ref, wgu_ref, wd_ref, gf_ref, o_ref, *, final):
    def up(rows):
        hn = _rmsnorm_rows(h_ref[rows, :], g_ref[...]).astype(BF16)
        gu = jnp.dot(hn, wgu_ref[...], preferred_element_type=F32)
        g, u = gu[:, :FF], gu[:, FF:]
        return (g * _sigmoid(g) * u).astype(BF16)

    def down(rows, act):
        y = h_ref[rows, :] + jnp.dot(act, wd_ref[...], preferred_element_type=F32)
        if final:
            y = _rmsnorm_rows(y, gf_ref[...])
        o_ref[rows, :] = y

    chunks = [slice(r, r + FFN_ROWS) for r in range(0, o_ref.shape[0], FFN_ROWS)]
    prev = None
    for rows in chunks:
        act = up(rows)
        if prev is not None:
            down(*prev)
        prev = (rows, act)
    down(*prev)


def _ffn(h, g, w_gate_up, w_down, layer, g_final, final):
    B, S, D = h.shape
    T = B * S
    tm = min(FFN_TM, T)
    resident = lambda a: pl.BlockSpec((None,) + a.shape[1:], lambda i: (layer, 0, 0),
                                      pipeline_mode=pl.Buffered(1))
    out = pl.pallas_call(
        functools.partial(_ffn_kernel, final=final),
        out_shape=jax.ShapeDtypeStruct((T, D), F32),
        grid=(T // tm,),
        in_specs=[
            pl.BlockSpec((tm, D), lambda i: (i, 0)),
            pl.BlockSpec((1, D), lambda i: (0, 0)),
            resident(w_gate_up),
            resident(w_down),
            pl.BlockSpec((1, D), lambda i: (0, 0)),
        ],
        out_specs=pl.BlockSpec((tm, D), lambda i: (i, 0)),
        compiler_params=pltpu.CompilerParams(
            dimension_semantics=("parallel",), vmem_limit_bytes=VMEM_LIMIT),
        name="ffn_final" if final else "ffn",
    )(h.reshape(T, D), g, w_gate_up, w_down, g_final)
    return out.reshape(B, S, D)


def _rope_tables(S):
    half = RET_DK // 2
    pos = jnp.arange(S, dtype=F32)
    inv = 1.0 / (ROPE_BASE ** (jnp.arange(half, dtype=F32) / half))
    ang = pos[:, None] * inv[None, :]
    cos = jnp.cos(ang)
    sin = jnp.sin(ang)
    return jnp.concatenate([cos, cos], axis=-1), jnp.concatenate([-sin, sin], axis=-1)


def kernel(x, norm_mix, w_in, w_ret_o, w_sb_o, w_out, norm_ffn, w_gate_up, w_down, norm_final):
    B, S, D = x.shape
    depth = w_in.shape[0]
    cos, sin = _rope_tables(S)
    log_g = jnp.log1p(-jnp.exp2(-5.0 - jnp.arange(RET_HEADS, dtype=F32)))
    log_g = jnp.broadcast_to(log_g[:, None, None], (RET_HEADS, 1, LANES))
    r = jnp.arange(SB_BLOCK)[:, None]
    c = jnp.arange(SB_BLOCK)[None, :]
    tri = (r >= c).astype(BF16)
    g_final = norm_final.reshape(1, D)
    w_in_bf16 = w_in.astype(BF16)
    w_gate_up_bf16 = w_gate_up.astype(BF16)
    w_down_bf16 = w_down.astype(BF16)

    h = x
    for layer in range(depth):
        proj = _inproj(h, norm_mix[layer].reshape(1, D), w_in_bf16, layer, cos, sin)
        ret = _retention(proj, log_g)
        sb = _stick_breaking(proj, tri)
        h = _merge(h, ret, sb, proj, w_ret_o, w_sb_o, w_out, layer)
        h = _ffn(h, norm_ffn[layer].reshape(1, D), w_gate_up_bf16, w_down_bf16, layer, g_final,
                 final=(layer == depth - 1))
    return h
```
